```python
import jax
import jax.numpy as jnp
from jax import lax
import numpy as np

D_MODEL = 2048
BATCH = 2
SEQ = 4096
DEPTH = 4
DEC_BATCH = 128
DEC_SEQ = 4
PAST_LEN = 8192
PAGE_SIZE = 128

N_EVEN = (DEPTH + 1) // 2
N_ODD = DEPTH // 2
EPS = 1e-6
ROPE_THETA = 10000.0
Q_BLOCK = 128
ADA_SCALE = 0.3

MLA_H = 8
MLA_NOPE = 128
MLA_ROPE = 64
MLA_V = 128
MLA_Q_LORA = 512
MLA_KV_LORA = 256
MLA_W = MLA_H * MLA_V
MLA_SCALE = (MLA_NOPE + MLA_ROPE) ** -0.5

FOX_H = 8
FOX_KVH = 2
FOX_G = FOX_H // FOX_KVH
FOX_DH = 128
FOX_W = FOX_H * FOX_DH
FOX_F_BIAS = 3.0
FOX_SCALE = FOX_DH ** -0.5

NSA_H = 16
NSA_DH = 128
NSA_W = NSA_H * NSA_DH
NSA_CMP = 32
NSA_SEL = 64
NSA_TOPK = 16
NSA_WIN = 512
NSA_FORCE = 1000.0
NSA_SCALE = NSA_DH ** -0.5

EV_SPLITS = (MLA_Q_LORA, MLA_KV_LORA, MLA_ROPE, MLA_W, FOX_W, FOX_KVH * FOX_DH, FOX_KVH * FOX_DH, FOX_H, FOX_W)
EV_IN = sum(EV_SPLITS)
EV_MIX = MLA_W + FOX_W
OD_SPLITS = (NSA_W,) + (NSA_DH,) * 6 + (NSA_H * 3, NSA_W)
OD_IN = sum(OD_SPLITS)

kernel_name = 'hybrid_mla_fox_nsa_decoder_step'


def split_cols(x, sizes):
    return jnp.split(x, [int(v) for v in np.cumsum(sizes)[:-1]], axis=-1)


def rmsnorm(x, g):
    xf = x.astype(jnp.float32)
    y = xf * lax.rsqrt(jnp.mean(xf * xf, axis=-1, keepdims=True) + EPS)
    return (y * g.astype(jnp.float32)).astype(x.dtype)


def rope(x, pos):
    half = x.shape[-1] // 2
    inv = ROPE_THETA ** (-jnp.arange(half, dtype=jnp.float32) / half)
    ang = pos.astype(jnp.float32)[:, None] * inv[None, :]
    cos = jnp.cos(ang)[:, None, :]
    sin = jnp.sin(ang)[:, None, :]
    xf = x.astype(jnp.float32)
    x1, x2 = xf[..., :half], xf[..., half:]
    return jnp.concatenate([x1 * cos - x2 * sin, x1 * sin + x2 * cos], axis=-1).astype(x.dtype)


def masked_softmax(s, mask):
    s = jnp.where(mask, s.astype(jnp.float32), -jnp.inf)
    m = jnp.max(s, axis=-1, keepdims=True)
    m = jnp.where(jnp.isfinite(m), m, 0.0)
    p = jnp.exp(s - m)
    return p / jnp.maximum(jnp.sum(p, axis=-1, keepdims=True), 1e-30)


def unblock(y):
    y = jnp.moveaxis(y, 0, 1)
    return y.reshape(y.shape[0], y.shape[1] * y.shape[2], *y.shape[3:])


def qblock(x, i):
    return lax.dynamic_slice_in_dim(x, i * Q_BLOCK, Q_BLOCK, axis=1)


def gather_pages(pool, layer, page_table):
    g = pool[layer, page_table]
    return g.reshape(g.shape[0], g.shape[1] * g.shape[2], *g.shape[3:])


def block_mean(x, size):
    b, n = x.shape[:2]
    return x.astype(jnp.float32).reshape(b, n // size, size, *x.shape[2:]).mean(axis=2).astype(x.dtype)


def modulate(x, c, ada_w, ada_b, g):
    shift, scale, gate = jnp.split(c @ ada_w + ada_b, 3, axis=-1)
    h = rmsnorm(x, g) * (1.0 + scale[:, None, :]) + shift[:, None, :]
    return h, gate[:, None, :]


def mla_attend(q_lat, q_rope, c_kv, k_rope, mask):
    s = (jnp.einsum('bqhr,bkr->bhqk', q_lat, c_kv) + jnp.einsum('bqhd,bkd->bhqk', q_rope, k_rope)) * MLA_SCALE
    p = masked_softmax(s, mask)
    return jnp.einsum('bhqk,bkr->bqhr', p.astype(c_kv.dtype), c_kv)


def mla_prompt(q_lat, q_rope, c_kv, k_rope):
    s_len = q_lat.shape[1]
    kpos = jnp.arange(s_len)

    def block(i):
        qpos = i * Q_BLOCK + jnp.arange(Q_BLOCK)
        return mla_attend(qblock(q_lat, i), qblock(q_rope, i), c_kv, k_rope, kpos[None, :] <= qpos[:, None])

    return unblock(lax.map(block, jnp.arange(s_len // Q_BLOCK)))


def fox_attend(q, k, v, cq, ck, mask):
    b, tq = q.shape[:2]
    tk = k.shape[1]
    qg = q.reshape(b, tq, FOX_KVH, FOX_G, FOX_DH)
    s = jnp.einsum('bqngd,bsnd->bngqs', qg, k).astype(jnp.float32) * FOX_SCALE
    cq_ = jnp.moveaxis(cq.reshape(b, tq, FOX_KVH, FOX_G), 1, -1)[..., :, None]
    ck_ = jnp.moveaxis(ck.reshape(b, tk, FOX_KVH, FOX_G), 1, -1)[..., None, :]
    p = masked_softmax(s + (cq_ - ck_), mask)
    o = jnp.einsum('bngqs,bsnd->bqngd', p.astype(v.dtype), v)
    return o.reshape(b, tq, FOX_H, FOX_DH)


def fox_prompt(q, k, v, csum):
    s_len = q.shape[1]
    kpos = jnp.arange(s_len)

    def block(i):
        qpos = i * Q_BLOCK + jnp.arange(Q_BLOCK)
        return fox_attend(qblock(q, i), k, v, qblock(csum, i), csum, kpos[None, :] <= qpos[:, None])

    return unblock(lax.map(block, jnp.arange(s_len // Q_BLOCK)))


def even_project(h, pos, w_in, q_norm, kv_norm, w_uq, w_uk, f_bias):
    b, t = h.shape[:2]
    cq, ckv, kr, g_mla, fq, fk, fv, fl, g_fox = split_cols(h @ w_in, EV_SPLITS)
    q = jnp.einsum('btr,rhd->bthd', rmsnorm(cq, q_norm), w_uq)
    q_lat = jnp.einsum('bthd,rhd->bthr', q[..., :MLA_NOPE], w_uk)
    q_rope = rope(q[..., MLA_NOPE:], pos)
    c_kv = rmsnorm(ckv, kv_norm)
    k_rope = rope(kr[:, :, None, :], pos)[:, :, 0]
    logf = jax.nn.log_sigmoid((fl + f_bias).astype(jnp.float32))
    fq = fq.reshape(b, t, FOX_H, FOX_DH)
    fk = fk.reshape(b, t, FOX_KVH, FOX_DH)
    fv = fv.reshape(b, t, FOX_KVH, FOX_DH)
    return q_lat, q_rope, c_kv, k_rope, g_mla, fq, fk, fv, logf, g_fox


def even_output(o_lat, o_fox, g_mla, g_fox, w_uv, w_out):
    b, t = o_lat.shape[:2]
    o_mla = jnp.einsum('bthr,rhd->bthd', o_lat, w_uv).reshape(b, t, MLA_W)
    mix = jnp.concatenate([o_mla * jax.nn.silu(g_mla), o_fox.reshape(b, t, FOX_W) * jax.nn.silu(g_fox)], axis=-1)
    return mix @ w_out


def nsa_cmp_branch(q, kcm, vcm, qpos):
    cend = (jnp.arange(kcm.shape[1]) + 1) * NSA_CMP - 1
    s = jnp.einsum('bqhd,bcd->bhqc', q, kcm) * NSA_SCALE
    p = masked_softmax(s, cend[None, :] <= qpos[:, None])
    return jnp.einsum('bhqc,bcd->bqhd', p.astype(vcm.dtype), vcm), p


def nsa_select(p_cmp, qpos, nsb):
    b, _, tq, nc = p_cmp.shape
    ratio = NSA_SEL // NSA_CMP
    imp = jnp.pad(jnp.sum(p_cmp, axis=1), ((0, 0), (0, 0), (0, nsb * ratio - nc)))
    imp = imp.reshape(b, tq, nsb, ratio).sum(-1)
    blk = jnp.arange(nsb)[None, :]
    cur = (qpos // NSA_SEL)[:, None]
    avail = blk * NSA_SEL <= qpos[:, None]
    forced = (blk == 0) | (blk == cur) | (blk == cur - 1)
    score = jnp.where(avail, imp + jnp.where(forced, NSA_FORCE, 0.0), -jnp.inf)
    score = jnp.pad(score, ((0, 0), (0, 0), (0, max(NSA_TOPK - nsb, 0))), constant_values=-jnp.inf)
    top, idx = lax.top_k(score, NSA_TOPK)
    return jnp.minimum(idx, nsb - 1), jnp.isfinite(top)


def nsa_sel_attend(q, ks, vs, idx, valid, qpos):
    b, tq = q.shape[:2]
    n = NSA_TOPK * NSA_SEL
    kpos = idx[..., None] * NSA_SEL + jnp.arange(NSA_SEL)
    mask = valid[..., None] & (kpos <= qpos[None, :, None, None])
    s = jnp.einsum('bqhd,bqmd->bqhm', q, ks.reshape(b, tq, n, NSA_DH)) * NSA_SCALE
    p = masked_softmax(s, mask.reshape(b, tq, 1, n))
    return jnp.einsum('bqhm,bqmd->bqhd', p.astype(vs.dtype), vs.reshape(b, tq, n, NSA_DH))


def nsa_win_attend(q, kw, vw, qpos, kpos):
    diff = qpos[:, None] - kpos[None, :]
    mask = (diff >= 0) & (diff < NSA_WIN) & (kpos[None, :] >= 0)
    s = jnp.einsum('bqhd,bkd->bhqk', q, kw) * NSA_SCALE
    p = masked_softmax(s, mask)
    return jnp.einsum('bhqk,bkd->bqhd', p.astype(vw.dtype), vw)


def nsa_prompt(q, kc, vc, ks, vs, kw, vw):
    b, s_len = q.shape[:2]
    kcm, vcm = block_mean(kc, NSA_CMP), block_mean(vc, NSA_CMP)
    nsb = s_len // NSA_SEL
    ks_b = ks.reshape(b, nsb, NSA_SEL, NSA_DH)
    vs_b = vs.reshape(b, nsb, NSA_SEL, NSA_DH)
    pad = ((0, 0), (NSA_WIN, 0), (0, 0))
    kw_p, vw_p = jnp.pad(kw, pad), jnp.pad(vw, pad)
    bidx = jnp.arange(b)[:, None, None]
    span = Q_BLOCK + NSA_WIN

    def block(i):
        start = i * Q_BLOCK
        qb = qblock(q, i)
        qpos = start + jnp.arange(Q_BLOCK)
        o_c, p_c = nsa_cmp_branch(qb, kcm, vcm, qpos)
        idx, valid = nsa_select(p_c, qpos, nsb)
        o_s = nsa_sel_attend(qb, ks_b[bidx, idx], vs_b[bidx, idx], idx, valid, qpos)
        kwb = lax.dynamic_slice_in_dim(kw_p, start, span, axis=1)
        vwb = lax.dynamic_slice_in_dim(vw_p, start, span, axis=1)
        o_w = nsa_win_attend(qb, kwb, vwb, qpos, start - NSA_WIN + jnp.arange(span))
        return o_c, o_s, o_w

    o_c, o_s, o_w = lax.map(block, jnp.arange(s_len // Q_BLOCK))
    return unblock(o_c), unblock(o_s), unblock(o_w)


def nsa_sample(q, kc, vc, ks, vs, kw, vw, pool_kc, pool_vc, pool_ks, pool_vs, buf_k, buf_v, layer, page_table):
    db, t = q.shape[:2]
    p_len = page_table.shape[1] * PAGE_SIZE
    n_tot = p_len + t
    qpos = p_len + jnp.arange(t)
    nc = n_tot // NSA_CMP
    kc_all = jnp.concatenate([gather_pages(pool_kc, layer, page_table), kc], axis=1)[:, :nc * NSA_CMP]
    vc_all = jnp.concatenate([gather_pages(pool_vc, layer, page_table), vc], axis=1)[:, :nc * NSA_CMP]
    o_c, p_c = nsa_cmp_branch(q, block_mean(kc_all, NSA_CMP), block_mean(vc_all, NSA_CMP), qpos)
    nsb = -(-n_tot // NSA_SEL)
    idx, valid = nsa_select(p_c, qpos, nsb)
    nps = p_len // NSA_SEL
    nnb = nsb - nps
    per_page = PAGE_SIZE // NSA_SEL
    bidx = jnp.arange(db)[:, None, None]
    jp = jnp.minimum(idx, nps - 1)
    phys = page_table[bidx, jp // per_page][..., None]
    rows = (jp % per_page)[..., None] * NSA_SEL + jnp.arange(NSA_SEL)
    jn = jnp.clip(idx - nps, 0, nnb - 1)
    in_past = (idx < nps)[..., None, None]

    def pick(pool, new):
        new_b = jnp.pad(new, ((0, 0), (0, nnb * NSA_SEL - t), (0, 0))).reshape(db, nnb, NSA_SEL, NSA_DH)
        return jnp.where(in_past, pool[layer, phys, rows], new_b[bidx, jn])

    o_s = nsa_sel_attend(q, pick(pool_ks, ks), pick(pool_vs, vs), idx, valid, qpos)
    wb = buf_k.shape[1]
    kw_all = jnp.concatenate([buf_k, kw], axis=1)
    vw_all = jnp.concatenate([buf_v, vw], axis=1)
    o_w = nsa_win_attend(q, kw_all, vw_all, qpos, p_len - wb + jnp.arange(wb + t))
    return o_c, o_s, o_w, kw_all[:, -wb:], vw_all[:, -wb:]


def odd_project(h, pos, w_in, br_bias):
    b, t = h.shape[:2]
    q, kc, vc, ks, vs, kw, vw, gl, g = split_cols(h @ w_in, OD_SPLITS)
    q = rope(q.reshape(b, t, NSA_H, NSA_DH), pos)
    kc, ks, kw = (rope(k[:, :, None, :], pos)[:, :, 0] for k in (kc, ks, kw))
    gates = jax.nn.sigmoid((gl + br_bias).astype(jnp.float32)).reshape(b, t, NSA_H, 3)
    return q, kc, vc, ks, vs, kw, vw, gates, g


def odd_output(o_c, o_s, o_w, gates, g, w_out):
    b, t = g.shape[:2]
    o = (gates[..., 0:1] * o_c.astype(jnp.float32) + gates[..., 1:2] * o_s.astype(jnp.float32)
         + gates[..., 2:3] * o_w.astype(jnp.float32))
    o = o.reshape(b, t, NSA_W).astype(g.dtype) * jax.nn.silu(g)
    return o @ w_out


def setup_inputs(seed: int = 0) -> dict:
    key = jax.random.key(seed)
    keys = iter(jax.random.split(key, 40))

    def nrm(shape, scale=1.0):
        return scale * jax.random.normal(next(keys), shape, jnp.float32)

    n_pages = PAST_LEN // PAGE_SIZE
    n_pool = (5 * DEC_BATCH * n_pages) // 4
    w_buf = min(NSA_WIN, PAST_LEN)
    perm = jax.random.permutation(next(keys), n_pool)
    page_table = perm[:DEC_BATCH * n_pages].reshape(DEC_BATCH, n_pages).astype(jnp.int32)
    pool_e = (N_EVEN, n_pool, PAGE_SIZE)
    pool_o = (N_ODD, n_pool, PAGE_SIZE)
    return {
        'x_prompt': nrm((BATCH, SEQ, D_MODEL)),
        'x_sample': nrm((DEC_BATCH, DEC_SEQ, D_MODEL)),
        'cache_mla_ckv': nrm(pool_e + (MLA_KV_LORA,)),
        'cache_mla_krope': nrm(pool_e + (MLA_ROPE,)),
        'cache_fox_k': nrm(pool_e + (FOX_KVH, FOX_DH)),
        'cache_fox_v': nrm(pool_e + (FOX_KVH, FOX_DH)),
        'cache_fox_logf': jax.nn.log_sigmoid(FOX_F_BIAS + nrm(pool_e + (FOX_H,))),
        'cache_nsa_cmp_k': nrm(pool_o + (NSA_DH,)),
        'cache_nsa_cmp_v': nrm(pool_o + (NSA_DH,)),
        'cache_nsa_sel_k': nrm(pool_o + (NSA_DH,)),
        'cache_nsa_sel_v': nrm(pool_o + (NSA_DH,)),
        'state_nsa_win_k': nrm((N_ODD, DEC_BATCH, w_buf, NSA_DH)),
        'state_nsa_win_v': nrm((N_ODD, DEC_BATCH, w_buf, NSA_DH)),
        'page_table': page_table,
        'c_prompt': nrm((BATCH, D_MODEL)),
        'c_sample': nrm((DEC_BATCH, D_MODEL)),
        'ada_w': nrm((DEPTH, D_MODEL, 3 * D_MODEL), ADA_SCALE * D_MODEL ** -0.5),
        'ada_b': nrm((DEPTH, 3 * D_MODEL), 0.02),
        'norm_g': 1.0 + nrm((DEPTH, D_MODEL), 0.1),
        'final_g': 1.0 + nrm((D_MODEL,), 0.1),
        'ev_w_in': nrm((N_EVEN, D_MODEL, EV_IN), D_MODEL ** -0.5),
        'mla_q_norm': 1.0 + nrm((N_EVEN, MLA_Q_LORA), 0.1),
        'mla_kv_norm': 1.0 + nrm((N_EVEN, MLA_KV_LORA), 0.1),
        'mla_w_uq': nrm((N_EVEN, MLA_Q_LORA, MLA_H, MLA_NOPE + MLA_ROPE), MLA_Q_LORA ** -0.5),
        'mla_w_uk': nrm((N_EVEN, MLA_KV_LORA, MLA_H, MLA_NOPE), MLA_KV_LORA ** -0.5),
        'mla_w_uv': nrm((N_EVEN, MLA_KV_LORA, MLA_H, MLA_V), MLA_KV_LORA ** -0.5),
        'fox_f_bias': FOX_F_BIAS + nrm((N_EVEN, FOX_H), 0.1),
        'ev_w_out': nrm((N_EVEN, EV_MIX, D_MODEL), EV_MIX ** -0.5),
        'od_w_in': nrm((N_ODD, D_MODEL, OD_IN), D_MODEL ** -0.5),
        'nsa_gate_bias': nrm((N_ODD, NSA_H * 3), 0.02),
        'od_w_out': nrm((N_ODD, NSA_W, D_MODEL), NSA_W ** -0.5),
    }


def reference(x_prompt, x_sample, cache_mla_ckv, cache_mla_krope, cache_fox_k, cache_fox_v, cache_fox_logf,
              cache_nsa_cmp_k, cache_nsa_cmp_v, cache_nsa_sel_k, cache_nsa_sel_v, state_nsa_win_k, state_nsa_win_v,
              page_table, c_prompt, c_sample, ada_w, ada_b, norm_g, final_g, ev_w_in, mla_q_norm, mla_kv_norm,
              mla_w_uq, mla_w_uk, mla_w_uv, fox_f_bias, ev_w_out, od_w_in, nsa_gate_bias, od_w_out):
    s_len = x_prompt.shape[1]
    t_len = x_sample.shape[1]
    p_len = page_table.shape[1] * PAGE_SIZE
    pos_p = jnp.arange(s_len)
    pos_s = p_len + jnp.arange(t_len)
    mask_s = jnp.arange(p_len + t_len)[None, :] <= pos_s[:, None]
    names = ('mla_ckv', 'mla_krope', 'fox_k', 'fox_v', 'fox_logf', 'cmp_k', 'cmp_v', 'sel_k', 'sel_v', 'win_k', 'win_v')
    new = {n + sfx: [] for n in names for sfx in ('_p', '_s')}
    xp, xs = x_prompt, x_sample
    for l in range(DEPTH):
        hp, gate_p = modulate(xp, c_prompt, ada_w[l], ada_b[l], norm_g[l])
        hs, gate_s = modulate(xs, c_sample, ada_w[l], ada_b[l], norm_g[l])
        if l % 2 == 0:
            e = l // 2
            wts = (ev_w_in[e], mla_q_norm[e], mla_kv_norm[e], mla_w_uq[e], mla_w_uk[e], fox_f_bias[e])
            ql, qr, ckv, kr, g1, fq, fk, fv, lf, g2 = even_project(hp, pos_p, *wts)
            o_lat = mla_prompt(ql, qr, ckv, kr)
            o_fox = fox_prompt(fq, fk, fv, jnp.cumsum(lf, axis=1))
            yp = even_output(o_lat, o_fox, g1, g2, mla_w_uv[e], ev_w_out[e])
            for n, v in zip(names[:5], (ckv, kr, fk, fv, lf.astype(xp.dtype))):
                new[n + '_p'].append(v)
            ql, qr, ckv, kr, g1, fq, fk, fv, lf, g2 = even_project(hs, pos_s, *wts)
            ckv_all = jnp.concatenate([gather_pages(cache_mla_ckv, e, page_table), ckv], axis=1)
            kr_all = jnp.concatenate([gather_pages(cache_mla_krope, e, page_table), kr], axis=1)
            o_lat = mla_attend(ql, qr, ckv_all, kr_all, mask_s)
            c_past = jnp.cumsum(gather_pages(cache_fox_logf, e, page_table).astype(jnp.float32), axis=1)
            c_new = c_past[:, -1:] + jnp.cumsum(lf, axis=1)
            fk_all = jnp.concatenate([gather_pages(cache_fox_k, e, page_table), fk], axis=1)
            fv_all = jnp.concatenate([gather_pages(cache_fox_v, e, page_table), fv], axis=1)
            o_fox = fox_attend(fq, fk_all, fv_all, c_new, jnp.concatenate([c_past, c_new], axis=1), mask_s)
            ys = even_output(o_lat, o_fox, g1, g2, mla_w_uv[e], ev_w_out[e])
            for n, v in zip(names[:5], (ckv, kr, fk, fv, lf.astype(xs.dtype))):
                new[n + '_s'].append(v)
        else:
            od = l // 2
            q, kc, vc, ks, vs, kw, vw, gates, g = odd_project(hp, pos_p, od_w_in[od], nsa_gate_bias[od])
            o_c, o_s, o_w = nsa_prompt(q, kc, vc, ks, vs, kw, vw)
            yp = odd_output(o_c, o_s, o_w, gates, g, od_w_out[od])
            wbp = min(NSA_WIN, s_len)
            for n, v in zip(names[5:], (kc, vc, ks, vs, kw[:, s_len - wbp:], vw[:, s_len - wbp:])):
                new[n + '_p'].append(v)
            q, kc, vc, ks, vs, kw, vw, gates, g = odd_project(hs, pos_s, od_w_in[od], nsa_gate_bias[od])
            o_c, o_s, o_w, bk, bv = nsa_sample(q, kc, vc, ks, vs, kw, vw, cache_nsa_cmp_k, cache_nsa_cmp_v,
                                               cache_nsa_sel_k, cache_nsa_sel_v, state_nsa_win_k[od],
                                               state_nsa_win_v[od], od, page_table)
            ys = odd_output(o_c, o_s, o_w, gates, g, od_w_out[od])
            for n, v in zip(names[5:], (kc, vc, ks, vs, bk, bv)):
                new[n + '_s'].append(v)
        xp = xp + gate_p * yp
        xs = xs + gate_s * ys
    y_prompt = rmsnorm(xp, final_g)
    y_sample = rmsnorm(xs, final_g)
    st = {k: jnp.stack(v) for k, v in new.items()}
    return (y_prompt, y_sample, st['mla_ckv_p'], st['mla_ckv_s'], st['mla_krope_p'], st['mla_krope_s'],
            st['fox_k_p'], st['fox_k_s'], st['fox_v_p'], st['fox_v_s'], st['fox_logf_p'], st['fox_logf_s'],
            st['cmp_k_p'], st['cmp_k_s'], st['cmp_v_p'], st['cmp_v_s'], st['sel_k_p'], st['sel_k_s'],
            st['sel_v_p'], st['sel_v_s'], st['win_k_p'], st['win_k_s'], st['win_v_p'], st['win_v_s'])
```

```python
import functools

import numpy as np
import jax
import jax.numpy as jnp
from jax import lax
from jax.experimental import pallas as pl
from jax.experimental.pallas import tpu as pltpu

F32 = jnp.float32
BF16 = jnp.bfloat16
NEG = -1e30
VMEM_LIMIT = 48 * 1024 * 1024
LANES = 128
SUBLANES = 8

EPS = 1e-6
ROPE_THETA = 10000.0
PAGE = 128
MLA_H, MLA_NOPE, MLA_ROPE, MLA_V, MLA_QL, MLA_KVL = 8, 128, 64, 128, 512, 256
MLA_SCALE = (MLA_NOPE + MLA_ROPE) ** -0.5
FOX_H, FOX_KVH, FOX_DH = 8, 2, 128
FOX_G = FOX_H // FOX_KVH
FOX_SCALE = FOX_DH ** -0.5
NSA_H, NSA_DH, NSA_CMP, NSA_SEL, NSA_TOPK, NSA_WIN = 16, 128, 32, 64, 16, 512
NSA_FORCE = 1000.0
NSA_SCALE = NSA_DH ** -0.5
SEL_SHIFT = 6
TPAD = SUBLANES


def _cparams(*sem):
    return pltpu.CompilerParams(dimension_semantics=sem, vmem_limit_bytes=VMEM_LIMIT)


def _dot_nt(a, b):
    return lax.dot_general(a, b, (((1,), (1,)), ((), ())), preferred_element_type=F32)


def _round_up(n, m):
    return -(-n // m) * m


def _mm_body(x_ref, w_ref, *rest, has_bias, has_res):
    o_ref = rest[-1]
    acc = jnp.dot(x_ref[0].astype(BF16), w_ref[...].astype(BF16), preferred_element_type=F32)
    k = 0
    if has_bias:
        acc = acc + rest[k][...]
        k += 1
    if has_res:
        acc = rest[k][0] + rest[k + 1][0] * acc
    o_ref[0] = acc.astype(o_ref.dtype)


def mm(x, w, *, w_idx=None, bias=None, res=None, gate=None, out_dtype=F32, tm=512, tn=512):
    G, R, K = x.shape
    N = w.shape[-1]
    tm = min(tm, R)
    tn = min(tn, N)
    assert R % tm == 0 and N % tn == 0, (R, tm, N, tn)
    in_specs = [pl.BlockSpec((1, tm, K), lambda g, i, j: (g, i, 0))]
    if w.ndim == 3:
        in_specs.append(pl.BlockSpec((None, K, tn), lambda g, i, j: (w_idx, 0, j)))
    else:
        in_specs.append(pl.BlockSpec((K, tn), lambda g, i, j: (0, j)))
    args = [x, w]
    if bias is not None:
        in_specs.append(pl.BlockSpec((1, tn), lambda g, i, j: (0, j)))
        args.append(bias)
    if res is not None:
        in_specs.append(pl.BlockSpec((1, tm, tn), lambda g, i, j: (g, i, j)))
        args.append(res)
        if gate.shape[1] == 1:
            in_specs.append(pl.BlockSpec((1, 1, tn), lambda g, i, j: (g, 0, j)))
        else:
            in_specs.append(pl.BlockSpec((1, tm, tn), lambda g, i, j: (g, i, j)))
        args.append(gate)
    return pl.pallas_call(
        functools.partial(_mm_body, has_bias=bias is not None, has_res=res is not None),
        grid=(G, R // tm, N // tn),
        in_specs=in_specs,
        out_specs=pl.BlockSpec((1, tm, tn), lambda g, i, j: (g, i, j)),
        out_shape=jax.ShapeDtypeStruct((G, R, N), out_dtype),
        compiler_params=_cparams("parallel", "parallel", "arbitrary"),
        name="mm",
    )(*args)


def _mm_heads_body(x_ref, w_ref, o_ref):
    o_ref[...] = jnp.dot(x_ref[...].astype(BF16), w_ref[0].astype(BF16),
                         preferred_element_type=F32).astype(o_ref.dtype)


def mm_heads(x, w, *, col_off=0, out_dtype=F32, tm=512):
    M = x.shape[0]
    H, dk, dn = w.shape
    tm = min(tm, M)
    assert M % tm == 0
    return pl.pallas_call(
        _mm_heads_body,
        grid=(M // tm, H),
        in_specs=[pl.BlockSpec((tm, dk), lambda i, h: (i, col_off + h)),
                  pl.BlockSpec((1, dk, dn), lambda i, h: (h, 0, 0))],
        out_specs=pl.BlockSpec((tm, dn), lambda i, h: (i, h)),
        out_shape=jax.ShapeDtypeStruct((M, H * dn), out_dtype),
        compiler_params=_cparams("parallel", "arbitrary"),
        name="mm_heads",
    )(x, w)


def _norm_body(x_ref, g_ref, *rest, mod):
    o_ref = rest[-1]
    xf = x_ref[0].astype(F32)
    y = xf * lax.rsqrt(jnp.mean(xf * xf, axis=-1, keepdims=True) + EPS) * g_ref[...]
    if mod:
        y = y * (1.0 + rest[0][0]) + rest[1][0]
    o_ref[0] = y.astype(o_ref.dtype)


def norm(x, g, scale=None, shift=None, *, col=0, width=None, out_dtype=F32, tr=256):
    G, R, _ = x.shape
    width = width or x.shape[-1]
    tr = min(tr, R)
    assert R % tr == 0
    in_specs = [pl.BlockSpec((1, tr, width), lambda a, i: (a, i, col)),
                pl.BlockSpec((1, width), lambda a, i: (0, 0))]
    args = [x, g.reshape(1, width)]
    mod = scale is not None
    if mod:
        for arr in (scale, shift):
            if arr.shape[1] == 1:
                in_specs.append(pl.BlockSpec((1, 1, width), lambda a, i: (a, 0, 0)))
            else:
                in_specs.append(pl.BlockSpec((1, tr, width), lambda a, i: (a, i, 0)))
            args.append(arr)
    return pl.pallas_call(
        functools.partial(_norm_body, mod=mod),
        grid=(G, R // tr),
        in_specs=in_specs,
        out_specs=pl.BlockSpec((1, tr, width), lambda a, i: (a, i, 0)),
        out_shape=jax.ShapeDtypeStruct((G, R, width), out_dtype),
        compiler_params=_cparams("parallel", "arbitrary"),
        name="norm",
    )(*args)


def _kv_hi(qi, tq, tk):
    return (qi * tq + tq - 1) // tk


def _kv_lo(qi, tq, tk, window):
    if not window:
        return 0
    if isinstance(qi, int):
        return max(qi * tq - (window - 1), 0) // tk
    return jnp.maximum(qi * tq - (window - 1), 0) // tk


def _flash_body(*refs, H, tq, tk, dks, dv, scale, window, has_bias, has_bm, v_is_k0, nks):
    it = iter(refs)
    q_refs = [next(it) for _ in dks]
    k_refs = [next(it) for _ in dks]
    v_ref = None if v_is_k0 else next(it)
    cq_ref = ck_ref = bm_ref = None
    if has_bias:
        cq_ref, ck_ref = next(it), next(it)
    if has_bm:
        bm_ref = next(it)
    o_ref = next(it)
    qs_refs = [next(it) for _ in dks]
    m_ref, l_ref, acc_ref, p_ref = next(it), next(it), next(it), next(it)

    qi = pl.program_id(2)
    ks = pl.program_id(3)
    kt = _kv_lo(qi, tq, tk, window) + ks

    @pl.when(ks == 0)
    def _init():
        for qr, qs, dk in zip(q_refs, qs_refs, dks):
            for h in range(H):
                qs[h * tq:(h + 1) * tq, :] = qr[0, :, h * dk:(h + 1) * dk].astype(BF16)
        m_ref[...] = jnp.full(m_ref.shape, NEG, F32)
        l_ref[...] = jnp.zeros(l_ref.shape, F32)
        acc_ref[...] = jnp.zeros(acc_ref.shape, F32)

    @pl.when(kt <= _kv_hi(qi, tq, tk))
    def _step():
        kbs = [kr[0].astype(BF16) for kr in k_refs]
        s_full = None
        for qs, kb in zip(qs_refs, kbs):
            d = _dot_nt(qs[...], kb)
            s_full = d if s_full is None else s_full + d
        vb = kbs[0] if v_is_k0 else v_ref[0].astype(BF16)
        qpos = qi * tq + lax.broadcasted_iota(jnp.int32, (tq, tk), 0)
        kpos = kt * tk + lax.broadcasted_iota(jnp.int32, (tq, tk), 1)
        mask = kpos <= qpos
        if window:
            mask = mask & ((qpos - kpos) < window)
        if has_bm:
            bm = bm_ref[0].astype(BF16)
            nc = bm.shape[1]
            cidx = lax.broadcasted_iota(jnp.int32, (nc, tk), 0)
            kk = kt * tk + lax.broadcasted_iota(jnp.int32, (nc, tk), 1)
            expand = jnp.where(cidx == 2 * lax.shift_right_logical(kk, SEL_SHIFT), 1.0, 0.0).astype(BF16)
            mask = mask & (jnp.dot(bm, expand, preferred_element_type=F32) > 0.5)
        for h in range(H):
            rows = slice(h * tq, (h + 1) * tq)
            s = s_full[rows] * scale
            if has_bias:
                s = s + (cq_ref[0, 0][:, h:h + 1] - ck_ref[0, 0][h:h + 1, :])
            s = jnp.where(mask, s, NEG)
            m_prev = m_ref[rows]
            m_new = jnp.maximum(m_prev, jnp.max(s, axis=-1, keepdims=True))
            alpha = jnp.exp(m_prev - m_new)
            p = jnp.exp(s - m_new)
            l_ref[rows] = alpha * l_ref[rows] + jnp.sum(p, axis=-1, keepdims=True)
            m_ref[rows] = m_new
            p_ref[rows] = p.astype(BF16)
            acc_ref[rows] = acc_ref[rows] * alpha
        acc_ref[...] += jnp.dot(p_ref[...], vb, preferred_element_type=F32)

    @pl.when(ks == nks - 1)
    def _fin():
        for h in range(H):
            rows = slice(h * tq, (h + 1) * tq)
            ok = m_ref[rows] > 0.5 * NEG
            o = jnp.where(ok, acc_ref[rows] / jnp.where(ok, l_ref[rows], 1.0), 0.0)
            o_ref[0, :, h * dv:(h + 1) * dv] = o.astype(o_ref.dtype)


def flash(qs, ks, v, *, H, G=1, dks, dv, scale, q_cols=None, k_cols=None, v_col=0, window=0,
          cq=None, ck=None, bm=None, tq=128, tk=512, out_dtype=F32):
    B, S = qs[0].shape[:2]
    n = len(dks)
    q_cols = q_cols or [0] * n
    k_cols = k_cols or [0] * n
    tq = min(tq, S)
    tk = min(tk, S)
    nq = S // tq
    nks = max(_kv_hi(i, tq, tk) - _kv_lo(i, tq, tk, window) + 1 for i in range(nq))

    def kblk(qi, s):
        return jnp.minimum(_kv_lo(qi, tq, tk, window) + s, _kv_hi(qi, tq, tk))

    in_specs, args = [], []
    for a, dk, c in zip(qs, dks, q_cols):
        in_specs.append(pl.BlockSpec((1, tq, H * dk), lambda b, g, i, s, c=c: (b, i, c + g)))
        args.append(a)
    for a, dk, c in zip(ks, dks, k_cols):
        in_specs.append(pl.BlockSpec((1, tk, dk), lambda b, g, i, s, c=c: (b, kblk(i, s), c + g)))
        args.append(a)
    if v is not None:
        in_specs.append(pl.BlockSpec((1, tk, dv), lambda b, g, i, s: (b, kblk(i, s), v_col + g)))
        args.append(v)
    if cq is not None:
        in_specs.append(pl.BlockSpec((1, 1, tq, H), lambda b, g, i, s: (b, g, i, 0)))
        in_specs.append(pl.BlockSpec((1, 1, H, tk), lambda b, g, i, s: (b, g, 0, kblk(i, s))))
        args += [cq, ck]
    if bm is not None:
        in_specs.append(pl.BlockSpec((1, tq, bm.shape[-1]), lambda b, g, i, s: (b, i, 0)))
        args.append(bm)
    scratch = [pltpu.VMEM((H * tq, dk), BF16) for dk in dks]
    scratch += [pltpu.VMEM((H * tq, 1), F32), pltpu.VMEM((H * tq, 1), F32),
                pltpu.VMEM((H * tq, dv), F32), pltpu.VMEM((H * tq, tk), BF16)]
    return pl.pallas_call(
        functools.partial(_flash_body, H=H, tq=tq, tk=tk, dks=tuple(dks), dv=dv, scale=scale,
                          window=window, has_bias=cq is not None, has_bm=bm is not None,
                          v_is_k0=v is None, nks=nks),
        grid=(B, G, nq, nks),
        in_specs=in_specs,
        out_specs=pl.BlockSpec((1, tq, H * dv), lambda b, g, i, s: (b, i, g)),
        out_shape=jax.ShapeDtypeStruct((B, S, G * H * dv), out_dtype),
        scratch_shapes=scratch,
        compiler_params=_cparams("parallel", "parallel", "parallel", "arbitrary"),
        name="flash",
    )(*args)


def _cmp_topk_body(q_ref, km_ref, vm_ref, oc_ref, bm_ref, p_ref, *, H, tq, ncp, pos0, tvalid, stacked):
    qi = pl.program_id(1)
    if stacked:
        qs = q_ref[0].astype(BF16)
    else:
        qs = jnp.concatenate([q_ref[0, :, h * NSA_DH:(h + 1) * NSA_DH] for h in range(H)],
                             axis=0).astype(BF16)
    s_full = _dot_nt(qs, km_ref[0].astype(BF16))
    r = lax.broadcasted_iota(jnp.int32, (tq, ncp), 0)
    c = lax.broadcasted_iota(jnp.int32, (tq, ncp), 1)
    if tvalid is None:
        pos = pos0 + qi * tq + r
    else:
        pos = pos0 + jnp.minimum(r, tvalid - 1)
    cmask = (c + 1) * NSA_CMP - 1 <= pos
    imp = jnp.zeros((tq, ncp), F32)
    for h in range(H):
        rows = slice(h * tq, (h + 1) * tq)
        s = jnp.where(cmask, s_full[rows] * NSA_SCALE, NEG)
        m = jnp.max(s, axis=-1, keepdims=True)
        p = jnp.where(cmask, jnp.exp(s - m), 0.0)
        p = p / jnp.maximum(jnp.sum(p, axis=-1, keepdims=True), 1e-30)
        imp = imp + p
        p_ref[rows] = p.astype(BF16)
    o = jnp.dot(p_ref[...], vm_ref[0].astype(BF16), preferred_element_type=F32)
    if stacked:
        oc_ref[0] = o.astype(oc_ref.dtype)
    else:
        for h in range(H):
            oc_ref[0, :, h * NSA_DH:(h + 1) * NSA_DH] = o[h * tq:(h + 1) * tq].astype(oc_ref.dtype)
    nxt = pltpu.roll(imp, ncp - 1, axis=1)
    prv = pltpu.roll(imp, 1, axis=1)
    ps = imp + jnp.where((c & 1) == 0, nxt, prv)
    blk = lax.shift_right_logical(c, 1)
    avail = blk * NSA_SEL <= pos
    cur = lax.shift_right_logical(pos, SEL_SHIFT)
    forced = (blk == 0) | (blk == cur) | (blk == cur - 1)
    sc = jnp.where(avail, ps + jnp.where(forced, NSA_FORCE, 0.0), NEG)
    rank = jnp.zeros((tq, ncp), F32)
    for i in range(ncp // 2):
        col = sc[:, 2 * i:2 * i + 1]
        beats = (col > sc) | ((col == sc) & (blk > i))
        rank = rank + jnp.where(beats, 1.0, 0.0)
    bm_ref[0] = jnp.where(avail & (rank < NSA_TOPK), 1.0, 0.0)


def cmp_topk(q, kmean, vmean, *, tq, pos0, tvalid=None, stacked=False):
    B = q.shape[0]
    ncp = kmean.shape[1]
    H = NSA_H
    if stacked:
        nq = 1
        q_spec = pl.BlockSpec((1, H * tq, NSA_DH), lambda b, i: (b, 0, 0))
        o_spec = pl.BlockSpec((1, H * tq, NSA_DH), lambda b, i: (b, 0, 0))
        o_shape = jax.ShapeDtypeStruct((B, H * tq, NSA_DH), F32)
    else:
        nq = q.shape[1] // tq
        q_spec = pl.BlockSpec((1, tq, H * NSA_DH), lambda b, i: (b, i, 0))
        o_spec = pl.BlockSpec((1, tq, H * NSA_DH), lambda b, i: (b, i, 0))
        o_shape = jax.ShapeDtypeStruct((B, nq * tq, H * NSA_DH), F32)
    return pl.pallas_call(
        functools.partial(_cmp_topk_body, H=H, tq=tq, ncp=ncp, pos0=pos0, tvalid=tvalid, stacked=stacked),
        grid=(B, nq),
        in_specs=[q_spec,
                  pl.BlockSpec((1, ncp, NSA_DH), lambda b, i: (b, 0, 0)),
                  pl.BlockSpec((1, ncp, NSA_DH), lambda b, i: (b, 0, 0))],
        out_specs=[o_spec, pl.BlockSpec((1, tq, ncp), lambda b, i: (b, i, 0))],
        out_shape=[o_shape, jax.ShapeDtypeStruct((B, nq * tq, ncp), F32)],
        scratch_shapes=[pltpu.VMEM((H * tq, ncp), BF16)],
        compiler_params=_cparams("parallel", "arbitrary"),
        name="cmp_topk",
    )(q, kmean, vmean)


def _block_mean_body(x_ref, o_ref, *, rows):
    x = x_ref[0].astype(F32)
    o_ref[0] = jnp.sum(x.reshape(rows // NSA_CMP, NSA_CMP, x.shape[-1]), axis=1) * (1.0 / NSA_CMP)


def block_mean(x, *, tr=1024):
    B, S, d = x.shape
    tr = min(tr, S)
    assert S % tr == 0 and (tr // NSA_CMP) % SUBLANES == 0
    return pl.pallas_call(
        functools.partial(_block_mean_body, rows=tr),
        grid=(B, S // tr),
        in_specs=[pl.BlockSpec((1, tr, d), lambda b, i: (b, i, 0))],
        out_specs=pl.BlockSpec((1, tr // NSA_CMP, d), lambda b, i: (b, i, 0)),
        out_shape=jax.ShapeDtypeStruct((B, S // NSA_CMP, d), F32),
        compiler_params=_cparams("parallel", "arbitrary"),
        name="block_mean",
    )(x)


def _pool_mean_body(pt_ref, *refs, pp):
    o_ref = refs[-1]
    per = PAGE // NSA_CMP
    for i in range(pp):
        x = refs[i][...].astype(F32)
        o_ref[0, i * per:(i + 1) * per, :] = jnp.sum(x.reshape(per, NSA_CMP, x.shape[-1]), axis=1) * (1.0 / NSA_CMP)


def pool_block_mean(pool, layer, pt_flat, db, npages, *, pp=8):
    d = pool.shape[-1]
    pp = min(pp, npages)
    assert npages % pp == 0
    per = PAGE // NSA_CMP
    in_specs = [pl.BlockSpec((None, None, PAGE, d),
                             lambda b, c, pt, i=i: (layer, pt[b * npages + c * pp + i], 0, 0))
                for i in range(pp)]
    return pl.pallas_call(
        functools.partial(_pool_mean_body, pp=pp),
        grid_spec=pltpu.PrefetchScalarGridSpec(
            num_scalar_prefetch=1, grid=(db, npages // pp), in_specs=in_specs,
            out_specs=pl.BlockSpec((1, pp * per, d), lambda b, c, pt: (b, c, 0))),
        out_shape=jax.ShapeDtypeStruct((db, npages * per, d), F32),
        compiler_params=_cparams("parallel", "arbitrary"),
        name="pool_block_mean",
    )(pt_flat, *([pool] * pp))


def _cumsum_body(*refs, paged, blk):
    if paged:
        _, x_ref, c_ref, ct_ref, carry_ref = refs
        x = x_ref[...].astype(F32)
    else:
        x_ref, c_ref, ct_ref, carry_ref = refs
        x = x_ref[0].astype(F32)
    nh = x.shape[-1]

    @pl.when(pl.program_id(1) == 0)
    def _():
        carry_ref[...] = jnp.zeros(carry_ref.shape, F32)

    r = lax.broadcasted_iota(jnp.int32, (blk, blk), 0)
    cc = lax.broadcasted_iota(jnp.int32, (blk, blk), 1)
    tri = jnp.where(cc <= r, 1.0, 0.0)
    c = jnp.dot(tri, x, preferred_element_type=F32, precision=lax.Precision.HIGHEST) + carry_ref[...]
    carry_ref[...] = c[blk - 1:blk, :]
    c_ref[0] = c
    eye = jnp.where(lax.broadcasted_iota(jnp.int32, (nh, nh), 0) == lax.broadcasted_iota(jnp.int32, (nh, nh), 1),
                    1.0, 0.0)
    ct_ref[0] = lax.dot_general(eye, c, (((1,), (1,)), ((), ())), preferred_element_type=F32,
                                precision=lax.Precision.HIGHEST)


def cumsum_seq(x):
    B, S, nh = x.shape
    blk = min(PAGE, S)
    return pl.pallas_call(
        functools.partial(_cumsum_body, paged=False, blk=blk),
        grid=(B, S // blk),
        in_specs=[pl.BlockSpec((1, blk, nh), lambda b, i: (b, i, 0))],
        out_specs=[pl.BlockSpec((1, blk, nh), lambda b, i: (b, i, 0)),
                   pl.BlockSpec((1, nh, blk), lambda b, i: (b, 0, i))],
        out_shape=[jax.ShapeDtypeStruct((B, S, nh), F32), jax.ShapeDtypeStruct((B, nh, S), F32)],
        scratch_shapes=[pltpu.VMEM((1, nh), F32)],
        compiler_params=_cparams("parallel", "arbitrary"),
        name="cumsum_seq",
    )(x)


def cumsum_pages(pool, layer, pt_flat, db, npages):
    nh = pool.shape[-1]
    return pl.pallas_call(
        functools.partial(_cumsum_body, paged=True, blk=PAGE),
        grid_spec=pltpu.PrefetchScalarGridSpec(
            num_scalar_prefetch=1, grid=(db, npages),
            in_specs=[pl.BlockSpec((None, None, PAGE, nh), lambda b, i, pt: (layer, pt[b * npages + i], 0, 0))],
            out_specs=[pl.BlockSpec((1, PAGE, nh), lambda b, i, pt: (b, i, 0)),
                       pl.BlockSpec((1, nh, PAGE), lambda b, i, pt: (b, 0, i))],
            scratch_shapes=[pltpu.VMEM((1, nh), F32)]),
        out_shape=[jax.ShapeDtypeStruct((db, npages * PAGE, nh), F32),
                   jax.ShapeDtypeStruct((db, nh, npages * PAGE), F32)],
        compiler_params=_cparams("parallel", "arbitrary"),
        name="cumsum_pages",
    )(pt_flat, pool)


def _decode_body(pt_ref, *refs, H, nparts, pp, dv, scale, window, has_bias, has_bm, v_is_k0,
                 nchunks, kpos0, qpos0, tvalid):
    it = iter(refs)
    q_refs = [next(it) for _ in range(nparts)]
    kp_refs = [[next(it) for _ in range(pp)] for _ in range(nparts)]
    vp_refs = kp_refs[0] if v_is_k0 else [next(it) for _ in range(pp)]
    kn_refs = [next(it) for _ in range(nparts)]
    vn_ref = kn_refs[0] if v_is_k0 else next(it)
    cq_ref = ck_ref = ckn_ref = bm_ref = None
    if has_bias:
        cq_ref, ck_ref, ckn_ref = next(it), next(it), next(it)
    if has_bm:
        bm_ref = next(it)
    o_ref = next(it)
    m_ref, l_ref, acc_ref = next(it), next(it), next(it)
    R = H * TPAD
    c = pl.program_id(2)

    @pl.when(c == 0)
    def _init():
        m_ref[...] = jnp.full(m_ref.shape, NEG, F32)
        l_ref[...] = jnp.zeros(l_ref.shape, F32)
        acc_ref[...] = jnp.zeros(acc_ref.shape, F32)

    qbs = [qr[0, 0].astype(BF16) for qr in q_refs]

    def update(s, kpos, ck, key_ok, pv):
        K = s.shape[-1]
        s3 = (s * scale).reshape(H, TPAD, K)
        if has_bias:
            s3 = s3 + (cq_ref[0, 0].reshape(H, TPAD, 1) - ck[:, None, :])
        t = jnp.minimum(lax.broadcasted_iota(jnp.int32, (TPAD, K), 0), tvalid - 1)
        qpos = qpos0 + t
        mask = kpos <= qpos
        if key_ok is not None:
            mask = mask & key_ok
        if window:
            mask = mask & ((qpos - kpos) < window)
        if has_bm:
            bm = bm_ref[0].astype(BF16)
            nc = bm.shape[1]
            cidx = lax.broadcasted_iota(jnp.int32, (nc, K), 0)
            kk = jnp.broadcast_to(kpos[0:1, :], (nc, K))
            expand = jnp.where(cidx == 2 * lax.shift_right_logical(kk, SEL_SHIFT), 1.0, 0.0).astype(BF16)
            mask = mask & (jnp.dot(bm, expand, preferred_element_type=F32) > 0.5)
        s3 = jnp.where(mask[None], s3, NEG)
        s2 = s3.reshape(R, K)
        m_prev = m_ref[...]
        m_new = jnp.maximum(m_prev, jnp.max(s2, axis=-1, keepdims=True))
        alpha = jnp.exp(m_prev - m_new)
        p = jnp.exp(s2 - m_new)
        l_ref[...] = alpha * l_ref[...] + jnp.sum(p, axis=-1, keepdims=True)
        m_ref[...] = m_new
        acc_ref[...] = acc_ref[...] * alpha + pv(p.astype(BF16))

    kbs = [[kp_refs[i][j][...].astype(BF16) for j in range(pp)] for i in range(nparts)]
    s_list = []
    for j in range(pp):
        s = None
        for i in range(nparts):
            d = _dot_nt(qbs[i], kbs[i][j])
            s = d if s is None else s + d
        s_list.append(s)
    s_all = jnp.concatenate(s_list, axis=1) if pp > 1 else s_list[0]
    K = pp * PAGE
    kpos = kpos0 + c * K + lax.broadcasted_iota(jnp.int32, (TPAD, K), 1)

    def pv_past(pb):
        out = None
        for j in range(pp):
            vb = kbs[0][j] if v_is_k0 else vp_refs[j][...].astype(BF16)
            d = jnp.dot(pb[:, j * PAGE:(j + 1) * PAGE], vb, preferred_element_type=F32)
            out = d if out is None else out + d
        return out

    update(s_all, kpos, ck_ref[0, 0] if has_bias else None, None, pv_past)

    @pl.when(c == nchunks - 1)
    def _new_and_fin():
        knb = [kn[0].astype(BF16) for kn in kn_refs]
        s = None
        for i in range(nparts):
            d = _dot_nt(qbs[i], knb[i])
            s = d if s is None else s + d
        j = lax.broadcasted_iota(jnp.int32, (TPAD, TPAD), 1)
        vnb = knb[0] if v_is_k0 else vn_ref[0].astype(BF16)
        update(s, qpos0 + j, ckn_ref[0, 0] if has_bias else None, j < tvalid,
               lambda pb: jnp.dot(pb, vnb, preferred_element_type=F32))
        ok = m_ref[...] > 0.5 * NEG
        o = jnp.where(ok, acc_ref[...] / jnp.where(ok, l_ref[...], 1.0), 0.0)
        o_ref[0, 0] = o.astype(o_ref.dtype)


def decode(qs, pools, vpool, layer, pt_flat, npages, knews, vnew, *, H, G=1, dks, dv, scale,
           qpos0, kpos0=0, tvalid, window=0, cq=None, ck=None, cknew=None, bm=None, pp=8):
    DB = qs[0].shape[0]
    nparts = len(dks)
    pp = min(pp, npages)
    assert npages % pp == 0
    nchunks = npages // pp
    R = H * TPAD
    in_specs, args = [], []
    for a, dk in zip(qs, dks):
        in_specs.append(pl.BlockSpec((1, 1, R, dk), lambda b, g, c, pt: (b, g, 0, 0)))
        args.append(a)

    def page_specs(dk):
        return [pl.BlockSpec((None, None, PAGE, dk),
                             lambda b, g, c, pt, j=j: (layer, pt[b * npages + c * pp + j], 0, g))
                for j in range(pp)]

    for a, dk in zip(pools, dks):
        in_specs += page_specs(dk)
        args += [a] * pp
    if vpool is not None:
        in_specs += page_specs(dv)
        args += [vpool] * pp
    for a, dk in zip(knews, dks):
        in_specs.append(pl.BlockSpec((1, TPAD, dk), lambda b, g, c, pt: (b, 0, g)))
        args.append(a)
    if vnew is not None:
        in_specs.append(pl.BlockSpec((1, TPAD, dv), lambda b, g, c, pt: (b, 0, g)))
        args.append(vnew)
    if cq is not None:
        in_specs.append(pl.BlockSpec((1, 1, R, 1), lambda b, g, c, pt: (b, g, 0, 0)))
        in_specs.append(pl.BlockSpec((1, 1, H, pp * PAGE), lambda b, g, c, pt: (b, g, 0, c)))
        in_specs.append(pl.BlockSpec((1, 1, H, TPAD), lambda b, g, c, pt: (b, g, 0, 0)))
        args += [cq, ck, cknew]
    if bm is not None:
        in_specs.append(pl.BlockSpec((1, TPAD, bm.shape[-1]), lambda b, g, c, pt: (b, 0, 0)))
        args.append(bm)
    return pl.pallas_call(
        functools.partial(_decode_body, H=H, nparts=nparts, pp=pp, dv=dv, scale=scale, window=window,
                          has_bias=cq is not None, has_bm=bm is not None, v_is_k0=vpool is None,
                          nchunks=nchunks, kpos0=kpos0, qpos0=qpos0, tvalid=tvalid),
        grid_spec=pltpu.PrefetchScalarGridSpec(
            num_scalar_prefetch=1, grid=(DB, G, nchunks), in_specs=in_specs,
            out_specs=pl.BlockSpec((1, 1, R, dv), lambda b, g, c, pt: (b, g, 0, 0)),
            scratch_shapes=[pltpu.VMEM((R, 1), F32), pltpu.VMEM((R, 1), F32), pltpu.VMEM((R, dv), F32)]),
        out_shape=jax.ShapeDtypeStruct((DB, G, R, dv), F32),
        compiler_params=_cparams("parallel", "parallel", "arbitrary"),
        name="decode",
    )(pt_flat, *args)


def _nsa_mix_body(oc_ref, os_ref, ow_ref, gt_ref, g_ref, o_ref):
    gt = gt_ref[...]
    for h in range(NSA_H):
        cols = slice(h * NSA_DH, (h + 1) * NSA_DH)
        o = (gt[:, h:h + 1] * oc_ref[:, cols] + gt[:, NSA_H + h:NSA_H + h + 1] * os_ref[:, cols]
             + gt[:, 2 * NSA_H + h:2 * NSA_H + h + 1] * ow_ref[:, cols])
        g = g_ref[:, cols]
        o_ref[:, cols] = (o * (g * jax.nn.sigmoid(g))).astype(o_ref.dtype)


def nsa_mix(oc, osel, ow, gates, proj, g_col, *, tm=256):
    M, W = oc.shape
    tm = min(tm, M)
    assert M % tm == 0
    spec = pl.BlockSpec((tm, W), lambda i: (i, 0))
    return pl.pallas_call(
        _nsa_mix_body,
        grid=(M // tm,),
        in_specs=[spec, spec, spec, pl.BlockSpec((tm, gates.shape[1]), lambda i: (i, 0)),
                  pl.BlockSpec((tm, W), lambda i: (i, g_col))],
        out_specs=spec,
        out_shape=jax.ShapeDtypeStruct((M, W), BF16),
        compiler_params=_cparams("parallel"),
        name="nsa_mix",
    )(oc, osel, ow, gates, proj)


def _shift_append_body(buf_ref, new_ref, o_ref, *, wb, t):
    o_ref[:, 0:wb - t, :] = buf_ref[:, t:wb, :]
    o_ref[:, wb - t:wb, :] = new_ref[:, 0:t, :]


def shift_append(state, layer, new, t, *, tb=8):
    _, DB, wb, d = state.shape
    tb = min(tb, DB)
    assert DB % tb == 0
    return pl.pallas_call(
        functools.partial(_shift_append_body, wb=wb, t=t),
        grid=(DB // tb,),
        in_specs=[pl.BlockSpec((None, tb, wb, d), lambda i: (layer, i, 0, 0)),
                  pl.BlockSpec((tb, TPAD, d), lambda i: (i, 0, 0))],
        out_specs=pl.BlockSpec((tb, wb, d), lambda i: (i, 0, 0)),
        out_shape=jax.ShapeDtypeStruct((DB, wb, d), state.dtype),
        compiler_params=_cparams("parallel"),
        name="shift_append",
    )(state, new)


def _rope(x, pos):
    half = x.shape[-1] // 2
    inv = ROPE_THETA ** (-jnp.arange(half, dtype=F32) / half)
    ang = pos.astype(F32)[:, None] * inv[None, :]
    cos = jnp.cos(ang)[:, None, :]
    sin = jnp.sin(ang)[:, None, :]
    x1, x2 = x[..., :half], x[..., half:]
    return jnp.concatenate([x1 * cos - x2 * sin, x1 * sin + x2 * cos], axis=-1)


def _pad_rows(x, n, axis):
    pad = [(0, 0)] * x.ndim
    pad[axis] = (0, n - x.shape[axis])
    return jnp.pad(x, pad)


def _stack_rows(x, H, d):
    DB, T, _ = x.shape
    x = x.reshape(DB, T, H, d).transpose(0, 2, 1, 3)
    return _pad_rows(x, TPAD, 2).reshape(DB, 1, H * TPAD, d)


def _unstack_rows(o, H, T):
    DB, _, d = o.shape
    return o.reshape(DB, H, TPAD, d)[:, :, :T].transpose(0, 2, 1, 3).reshape(DB, T, H * d)


def _even_weights(w_in, w_uq, w_uk, w_uv, w_out):
    D = w_in.shape[0]
    o = np.cumsum((0, MLA_QL, MLA_KVL, MLA_ROPE, MLA_H * MLA_V, FOX_H * FOX_DH, FOX_KVH * FOX_DH,
                   FOX_KVH * FOX_DH, FOX_H, FOX_H * FOX_DH))
    seg = lambda i: w_in[:, o[i]:o[i + 1]]
    order = [seg(4), seg(3), seg(8), seg(0), seg(1), seg(5), seg(6), seg(2), seg(7)]
    n = sum(s.shape[1] for s in order)
    w = jnp.concatenate(order + [jnp.zeros((D, _round_up(n, 512) - n), w_in.dtype)], axis=1).astype(BF16)
    uq = jnp.concatenate([w_uq[:, :, :MLA_NOPE].reshape(MLA_QL, -1), w_uq[:, :, MLA_NOPE:].reshape(MLA_QL, -1)],
                         axis=1).astype(BF16)
    uk = w_uk.transpose(1, 2, 0).astype(BF16)
    uv = w_uv.transpose(1, 0, 2).astype(BF16)
    return w, uq, uk, uv, w_out.astype(BF16)


EV_COL = dict(fq=0, gm=1024, gf=2048, cq=3072, ckv=3584, fk=3840, fv=4096, kr=4352, fl=4416)


def _odd_weights(w_in, gate_bias, w_out):
    D = w_in.shape[0]
    W = NSA_H * NSA_DH
    q, kv, gl, g = w_in[:, :W], w_in[:, W:W + 6 * NSA_DH], w_in[:, W + 6 * NSA_DH:W + 6 * NSA_DH + 3 * NSA_H], \
        w_in[:, W + 6 * NSA_DH + 3 * NSA_H:]
    gl = gl.reshape(D, NSA_H, 3).transpose(0, 2, 1).reshape(D, 3 * NSA_H)
    n = 2 * W + 6 * NSA_DH + 3 * NSA_H
    w = jnp.concatenate([q, g, kv, gl, jnp.zeros((D, _round_up(n, 512) - n), w_in.dtype)], axis=1).astype(BF16)
    gb = gate_bias.reshape(NSA_H, 3).T.reshape(3 * NSA_H)
    return w, gb, w_out.astype(BF16)


def kernel(x_prompt, x_sample, cache_mla_ckv, cache_mla_krope, cache_fox_k, cache_fox_v, cache_fox_logf, cache_nsa_cmp_k, cache_nsa_cmp_v, cache_nsa_sel_k, cache_nsa_sel_v, state_nsa_win_k, state_nsa_win_v, page_table, c_prompt, c_sample, ada_w, ada_b, norm_g, final_g, ev_w_in, mla_q_norm, mla_kv_norm, mla_w_uq, mla_w_uk, mla_w_uv, fox_f_bias, ev_w_out, od_w_in, nsa_gate_bias, od_w_out):
    B, S, D = x_prompt.shape
    DB, T, _ = x_sample.shape
    depth = ada_w.shape[0]
    npages = page_table.shape[1]
    P = npages * PAGE
    MS = DB * T
    pos_p = jnp.arange(S)
    pos_s = P + jnp.arange(T)
    pt_flat = page_table.reshape(-1).astype(jnp.int32)

    c_all = jnp.concatenate([c_prompt, c_sample], axis=0)
    mc = _round_up(B + DB, SUBLANES)
    c_all = _pad_rows(c_all, mc, 0)[None]
    fox_pool_k = cache_fox_k.reshape(cache_fox_k.shape[:3] + (FOX_KVH * FOX_DH,))
    fox_pool_v = cache_fox_v.reshape(cache_fox_v.shape[:3] + (FOX_KVH * FOX_DH,))

    xp = x_prompt
    xs = x_sample.reshape(1, MS, D)
    outs = {k: [] for k in ('ckv_p', 'ckv_s', 'kr_p', 'kr_s', 'fk_p', 'fk_s', 'fv_p', 'fv_s', 'lf_p', 'lf_s',
                            'kc_p', 'kc_s', 'vc_p', 'vc_s', 'ks_p', 'ks_s', 'vs_p', 'vs_s',
                            'wk_p', 'wk_s', 'wv_p', 'wv_s')}

    for l in range(depth):
        mod = mm(c_all, ada_w, w_idx=l, bias=ada_b[l][None])[0]
        shift, scale, gate = mod[:, :D], mod[:, D:2 * D], mod[:, 2 * D:]
        rep = lambda a: jnp.repeat(a[B:B + DB], T, axis=0)[None]
        hp = norm(xp, norm_g[l], scale[:B, None], shift[:B, None], out_dtype=BF16)
        hs = norm(xs, norm_g[l], rep(scale), rep(shift), out_dtype=BF16)
        gate_p, gate_s = gate[:B, None], rep(gate)

        if l % 2 == 0:
            e = l // 2
            w_in, w_uq, w_uk, w_uv, w_out = _even_weights(ev_w_in[e], mla_w_uq[e], mla_w_uk[e], mla_w_uv[e],
                                                          ev_w_out[e])
            C = EV_COL

            def project(h, pos):
                proj = mm(h, w_in)
                g_, r_, _ = proj.shape
                cqn = norm(proj, mla_q_norm[e], col=C['cq'] // MLA_QL, width=MLA_QL, out_dtype=BF16)
                q = mm(cqn, w_uq)
                q_lat = mm_heads(q.reshape(g_ * r_, -1), w_uk).reshape(g_, r_, -1)
                q_rope = _rope(q[..., MLA_H * MLA_NOPE:].reshape(g_, r_, MLA_H, MLA_ROPE), pos).reshape(g_, r_, -1)
                ckv = norm(proj, mla_kv_norm[e], col=C['ckv'] // MLA_KVL, width=MLA_KVL)
                kr = _rope(proj[..., None, C['kr']:C['kr'] + MLA_ROPE], pos)[..., 0, :]
                logf = jax.nn.log_sigmoid(proj[..., C['fl']:C['fl'] + FOX_H] + fox_f_bias[e])
                return proj, q_lat, q_rope, ckv, kr, logf

            def output(proj, o_lat, o_fox, x, gate_):
                g_, r_, _ = proj.shape
                o_mla = mm_heads(o_lat.reshape(g_ * r_, -1), w_uv).reshape(g_, r_, -1)
                gm = proj[..., C['gm']:C['gm'] + MLA_H * MLA_V]
                gf = proj[..., C['gf']:C['gf'] + FOX_H * FOX_DH]
                mix = jnp.concatenate([o_mla * jax.nn.silu(gm), o_fox * jax.nn.silu(gf)], axis=-1).astype(BF16)
                return mm(mix, w_out, res=x, gate=gate_)

            proj, q_lat, q_rope, ckv, kr, logf = project(hp, pos_p)
            o_lat = flash([q_lat, q_rope], [ckv, kr], None, H=MLA_H, dks=(MLA_KVL, MLA_ROPE), dv=MLA_KVL,
                          scale=MLA_SCALE)
            csum, csum_t = cumsum_seq(logf)
            cq = csum.reshape(B, S, FOX_KVH, FOX_G).transpose(0, 2, 1, 3)
            ck = csum_t.reshape(B, FOX_KVH, FOX_G, S)
            kb = C['fk'] // FOX_DH
            vb = C['fv'] // FOX_DH
            o_fox = flash([proj], [proj], proj, H=FOX_G, G=FOX_KVH, dks=(FOX_DH,), dv=FOX_DH, scale=FOX_SCALE,
                          q_cols=[0], k_cols=[kb], v_col=vb, cq=cq, ck=ck)
            fk = proj[..., C['fk']:C['fk'] + FOX_KVH * FOX_DH]
            fv = proj[..., C['fv']:C['fv'] + FOX_KVH * FOX_DH]
            outs['ckv_p'].append(ckv)
            outs['kr_p'].append(kr)
            outs['fk_p'].append(fk.reshape(B, S, FOX_KVH, FOX_DH))
            outs['fv_p'].append(fv.reshape(B, S, FOX_KVH, FOX_DH))
            outs['lf_p'].append(logf)
            xp = output(proj, o_lat, o_fox, xp, gate_p)

            proj, q_lat, q_rope, ckv, kr, logf = project(hs, jnp.tile(pos_s, DB))
            ckv3, kr3 = ckv.reshape(DB, T, -1), kr.reshape(DB, T, -1)
            o_lat = decode([_stack_rows(q_lat.reshape(DB, T, -1), MLA_H, MLA_KVL),
                            _stack_rows(q_rope.reshape(DB, T, -1), MLA_H, MLA_ROPE)],
                           [cache_mla_ckv, cache_mla_krope], None, e, pt_flat, npages,
                           [_pad_rows(ckv3, TPAD, 1), _pad_rows(kr3, TPAD, 1)], None,
                           H=MLA_H, dks=(MLA_KVL, MLA_ROPE), dv=MLA_KVL, scale=MLA_SCALE, qpos0=P, tvalid=T)
            o_lat = _unstack_rows(o_lat[:, 0], MLA_H, T).reshape(1, MS, -1)
            _, cpast_t = cumsum_pages(cache_fox_logf, e, pt_flat, DB, npages)
            lf3 = logf.reshape(DB, T, FOX_H)
            c_new = cpast_t[:, :, -1][:, None, :] + jnp.cumsum(lf3, axis=1)
            fq = proj[0, :, :FOX_H * FOX_DH].reshape(DB, T, FOX_KVH, FOX_G, FOX_DH).transpose(0, 2, 3, 1, 4)
            fq = _pad_rows(fq, TPAD, 3).reshape(DB, FOX_KVH, FOX_G * TPAD, FOX_DH)
            cqs = _pad_rows(c_new.reshape(DB, T, FOX_KVH, FOX_G).transpose(0, 2, 3, 1), TPAD, 3)
            fk = proj[0, :, C['fk']:C['fk'] + FOX_KVH * FOX_DH].reshape(DB, T, -1)
            fv = proj[0, :, C['fv']:C['fv'] + FOX_KVH * FOX_DH].reshape(DB, T, -1)
            o_fox = decode([fq], [fox_pool_k], fox_pool_v, e, pt_flat, npages,
                           [_pad_rows(fk, TPAD, 1)], _pad_rows(fv, TPAD, 1),
                           H=FOX_G, G=FOX_KVH, dks=(FOX_DH,), dv=FOX_DH, scale=FOX_SCALE, qpos0=P, tvalid=T,
                           cq=cqs.reshape(DB, FOX_KVH, FOX_G * TPAD, 1),
                           ck=cpast_t.reshape(DB, FOX_KVH, FOX_G, P), cknew=cqs)
            o_fox = o_fox.reshape(DB, FOX_KVH, FOX_G, TPAD, FOX_DH)[:, :, :, :T].transpose(0, 3, 1, 2, 4)
            o_fox = o_fox.reshape(1, MS, FOX_H * FOX_DH)
            outs['ckv_s'].append(ckv3)
            outs['kr_s'].append(kr3)
            outs['fk_s'].append(fk.reshape(DB, T, FOX_KVH, FOX_DH))
            outs['fv_s'].append(fv.reshape(DB, T, FOX_KVH, FOX_DH))
            outs['lf_s'].append(lf3)
            xs = output(proj, o_lat, o_fox, xs, gate_s)
        else:
            od = l // 2
            w_in, gbias, w_out = _odd_weights(od_w_in[od], nsa_gate_bias[od], od_w_out[od])
            W = NSA_H * NSA_DH
            KV0 = 2 * W

            def project(h, pos):
                proj = mm(h, w_in)
                g_, r_, _ = proj.shape
                q = _rope(proj[..., :W].reshape(g_, r_, NSA_H, NSA_DH), pos).reshape(g_, r_, W)
                kv = [proj[..., KV0 + i * NSA_DH:KV0 + (i + 1) * NSA_DH] for i in range(6)]
                kc, vc, ks, vs, kw, vw = kv
                kc, ks, kw = (_rope(k[..., None, :], pos)[..., 0, :] for k in (kc, ks, kw))
                gates = jax.nn.sigmoid(proj[..., KV0 + 6 * NSA_DH:KV0 + 6 * NSA_DH + 3 * NSA_H] + gbias)
                return proj, q, kc, vc, ks, vs, kw, vw, gates

            def output(proj, o_c, o_s, o_w, gates, x, gate_):
                g_, r_, _ = proj.shape
                mix = nsa_mix(o_c.reshape(g_ * r_, W), o_s.reshape(g_ * r_, W), o_w.reshape(g_ * r_, W),
                              gates.reshape(g_ * r_, -1), proj.reshape(g_ * r_, -1), 1)
                return mm(mix.reshape(g_, r_, W), w_out, res=x, gate=gate_)

            proj, q, kc, vc, ks, vs, kw, vw, gates = project(hp, pos_p)
            ncp = _round_up(S // NSA_CMP, LANES)
            kmean = _pad_rows(block_mean(kc, tr=min(1024, S)), ncp, 1)
            vmean = _pad_rows(block_mean(vc, tr=min(1024, S)), ncp, 1)
            tq = min(128, S)
            o_c, bm = cmp_topk(q, kmean, vmean, tq=tq, pos0=0)
            o_s = flash([q], [ks], vs, H=NSA_H, dks=(NSA_DH,), dv=NSA_DH, scale=NSA_SCALE, bm=bm, tq=tq)
            o_w = flash([q], [kw], vw, H=NSA_H, dks=(NSA_DH,), dv=NSA_DH, scale=NSA_SCALE, window=NSA_WIN,
                        tq=tq, tk=tq)
            wbp = min(NSA_WIN, S)
            for n_, v_ in zip(('kc_p', 'vc_p', 'ks_p', 'vs_p', 'wk_p', 'wv_p'),
                              (kc, vc, ks, vs, kw[:, S - wbp:], vw[:, S - wbp:])):
                outs[n_].append(v_)
            xp = output(proj, o_c, o_s, o_w, gates, xp, gate_p)

            proj, q, kc, vc, ks, vs, kw, vw, gates = project(hs, jnp.tile(pos_s, DB))
            r3 = lambda a: a.reshape(DB, T, -1)
            kc, vc, ks, vs, kw, vw = map(r3, (kc, vc, ks, vs, kw, vw))
            qst = _stack_rows(r3(q), NSA_H, NSA_DH)
            ncp = _round_up((P + T + NSA_SEL - 1) // NSA_SEL * 2, LANES)
            kmean = _pad_rows(pool_block_mean(cache_nsa_cmp_k, od, pt_flat, DB, npages), ncp, 1)
            vmean = _pad_rows(pool_block_mean(cache_nsa_cmp_v, od, pt_flat, DB, npages), ncp, 1)
            o_c, bm = cmp_topk(qst[:, 0], kmean, vmean, tq=TPAD, pos0=P, tvalid=T, stacked=True)
            pad8 = lambda a: _pad_rows(a, TPAD, 1)
            o_s = decode([qst], [cache_nsa_sel_k], cache_nsa_sel_v, od, pt_flat, npages, [pad8(ks)], pad8(vs),
                         H=NSA_H, dks=(NSA_DH,), dv=NSA_DH, scale=NSA_SCALE, qpos0=P, tvalid=T, bm=bm)
            wb = state_nsa_win_k.shape[2]
            wpages = wb // PAGE
            wtab = jnp.arange(DB * wpages, dtype=jnp.int32)
            wshape = (state_nsa_win_k.shape[0], DB * wpages, PAGE, NSA_DH)
            o_w = decode([qst], [state_nsa_win_k.reshape(wshape)], state_nsa_win_v.reshape(wshape), od, wtab,
                         wpages, [pad8(kw)], pad8(vw), H=NSA_H, dks=(NSA_DH,), dv=NSA_DH, scale=NSA_SCALE,
                         qpos0=P, kpos0=P - wb, tvalid=T, window=NSA_WIN, pp=wpages)
            un = lambda o: _unstack_rows(o, NSA_H, T).reshape(1, MS, W)
            bk = shift_append(state_nsa_win_k, od, pad8(kw), T)
            bv = shift_append(state_nsa_win_v, od, pad8(vw), T)
            for n_, v_ in zip(('kc_s', 'vc_s', 'ks_s', 'vs_s', 'wk_s', 'wv_s'), (kc, vc, ks, vs, bk, bv)):
                outs[n_].append(v_)
            xs = output(proj, un(o_c), un(o_s[:, 0]), un(o_w[:, 0]), gates, xs, gate_s)

    y_prompt = norm(xp, final_g)
    y_sample = norm(xs, final_g).reshape(DB, T, D)
    st = {k: jnp.stack(v) for k, v in outs.items()}
    return (y_prompt, y_sample, st['ckv_p'], st['ckv_s'], st['kr_p'], st['kr_s'],
            st['fk_p'], st['fk_s'], st['fv_p'], st['fv_s'], st['lf_p'], st['lf_s'],
            st['kc_p'], st['kc_s'], st['vc_p'], st['vc_s'], st['ks_p'], st['ks_s'],
            st['vs_p'], st['vs_s'], st['wk_p'], st['wk_s'], st['wv_p'], st['wv_s'])
```

```python
import functools
import math

import numpy as np
import jax
import jax.numpy as jnp
from jax import lax
from jax.experimental import pallas as pl
from jax.experimental.pallas import tpu as pltpu

F32 = jnp.float32
BF16 = jnp.bfloat16
NEG = -1e30
LOG2E = math.log2(math.e)
VMEM_LIMIT = 48 * 1024 * 1024
LANES = 128
SUBLANES = 8

EPS = 1e-6
ROPE_THETA = 10000.0
PAGE = 128
MLA_H, MLA_NOPE, MLA_ROPE, MLA_V, MLA_QL, MLA_KVL = 8, 128, 64, 128, 512, 256
MLA_SCALE = (MLA_NOPE + MLA_ROPE) ** -0.5
FOX_H, FOX_KVH, FOX_DH = 8, 2, 128
FOX_G = FOX_H // FOX_KVH
FOX_SCALE = FOX_DH ** -0.5
NSA_H, NSA_DH, NSA_CMP, NSA_SEL, NSA_TOPK, NSA_WIN = 16, 128, 32, 64, 16, 512
NSA_FORCE = 1000.0
NSA_SCALE = NSA_DH ** -0.5
SEL_SHIFT = 6
TPAD = SUBLANES
ROW_CHUNK = 16


def _cparams(*sem):
    return pltpu.CompilerParams(dimension_semantics=sem, vmem_limit_bytes=VMEM_LIMIT)


def _dot_nt(a, b):
    return lax.dot_general(a, b, (((1,), (1,)), ((), ())), preferred_element_type=F32)


def _round_up(n, m):
    return -(-n // m) * m


def _mm_body(x_ref, w_ref, *rest, has_bias, has_res):
    o_ref = rest[-1]
    acc = jnp.dot(x_ref[0].astype(BF16), w_ref[...].astype(BF16), preferred_element_type=F32)
    k = 0
    if has_bias:
        acc = acc + rest[k][...]
        k += 1
    if has_res:
        acc = rest[k][0] + rest[k + 1][0] * acc
    o_ref[0] = acc.astype(o_ref.dtype)


def mm(x, w, *, w_idx=None, bias=None, res=None, gate=None, out_dtype=F32, tm=512, tn=512):
    G, R, K = x.shape
    N = w.shape[-1]
    tm = min(tm, R)
    tn = min(tn, N)
    assert R % tm == 0 and N % tn == 0, (R, tm, N, tn)
    in_specs = [pl.BlockSpec((1, tm, K), lambda g, i, j: (g, i, 0))]
    if w.ndim == 3:
        in_specs.append(pl.BlockSpec((None, K, tn), lambda g, i, j: (w_idx, 0, j)))
    else:
        in_specs.append(pl.BlockSpec((K, tn), lambda g, i, j: (0, j)))
    args = [x, w]
    if bias is not None:
        in_specs.append(pl.BlockSpec((1, tn), lambda g, i, j: (0, j)))
        args.append(bias)
    if res is not None:
        in_specs.append(pl.BlockSpec((1, tm, tn), lambda g, i, j: (g, i, j)))
        args.append(res)
        if gate.shape[1] == 1:
            in_specs.append(pl.BlockSpec((1, 1, tn), lambda g, i, j: (g, 0, j)))
        else:
            in_specs.append(pl.BlockSpec((1, tm, tn), lambda g, i, j: (g, i, j)))
        args.append(gate)
    return pl.pallas_call(
        functools.partial(_mm_body, has_bias=bias is not None, has_res=res is not None),
        grid=(G, R // tm, N // tn),
        in_specs=in_specs,
        out_specs=pl.BlockSpec((1, tm, tn), lambda g, i, j: (g, i, j)),
        out_shape=jax.ShapeDtypeStruct((G, R, N), out_dtype),
        compiler_params=_cparams("parallel", "parallel", "arbitrary"),
        name="mm",
    )(*args)


def _mm_heads_body(x_ref, w_ref, o_ref):
    o_ref[...] = jnp.dot(x_ref[...].astype(BF16), w_ref[0].astype(BF16),
                         preferred_element_type=F32).astype(o_ref.dtype)


def mm_heads(x, w, *, col_off=0, out_dtype=F32, tm=512):
    M = x.shape[0]
    H, dk, dn = w.shape
    tm = min(tm, M)
    assert M % tm == 0
    return pl.pallas_call(
        _mm_heads_body,
        grid=(M // tm, H),
        in_specs=[pl.BlockSpec((tm, dk), lambda i, h: (i, col_off + h)),
                  pl.BlockSpec((1, dk, dn), lambda i, h: (h, 0, 0))],
        out_specs=pl.BlockSpec((tm, dn), lambda i, h: (i, h)),
        out_shape=jax.ShapeDtypeStruct((M, H * dn), out_dtype),
        compiler_params=_cparams("parallel", "arbitrary"),
        name="mm_heads",
    )(x, w)


def _norm_body(x_ref, g_ref, *rest, mod):
    o_ref = rest[-1]
    xf = x_ref[0].astype(F32)
    y = xf * lax.rsqrt(jnp.mean(xf * xf, axis=-1, keepdims=True) + EPS) * g_ref[...]
    if mod:
        y = y * (1.0 + rest[0][0]) + rest[1][0]
    o_ref[0] = y.astype(o_ref.dtype)


def norm(x, g, scale=None, shift=None, *, col=0, width=None, out_dtype=F32, tr=256):
    G, R, _ = x.shape
    width = width or x.shape[-1]
    tr = min(tr, R)
    assert R % tr == 0
    in_specs = [pl.BlockSpec((1, tr, width), lambda a, i: (a, i, col)),
                pl.BlockSpec((1, width), lambda a, i: (0, 0))]
    args = [x, g.reshape(1, width)]
    mod = scale is not None
    if mod:
        for arr in (scale, shift):
            if arr.shape[1] == 1:
                in_specs.append(pl.BlockSpec((1, 1, width), lambda a, i: (a, 0, 0)))
            else:
                in_specs.append(pl.BlockSpec((1, tr, width), lambda a, i: (a, i, 0)))
            args.append(arr)
    return pl.pallas_call(
        functools.partial(_norm_body, mod=mod),
        grid=(G, R // tr),
        in_specs=in_specs,
        out_specs=pl.BlockSpec((1, tr, width), lambda a, i: (a, i, 0)),
        out_shape=jax.ShapeDtypeStruct((G, R, width), out_dtype),
        compiler_params=_cparams("parallel", "arbitrary"),
        name="norm",
    )(*args)


def _kv_hi(qi, tq, tk):
    return (qi * tq + tq - 1) // tk


def _kv_lo(qi, tq, tk, window):
    if not window:
        return 0
    if isinstance(qi, int):
        return max(qi * tq - (window - 1), 0) // tk
    return jnp.maximum(qi * tq - (window - 1), 0) // tk


def _flash_body(*refs, H, tq, tk, dks, dv, c2, window, has_bias, has_bm, v_is_k0, nks):
    it = iter(refs)
    q_refs = [next(it) for _ in dks]
    k_refs = [next(it) for _ in dks]
    v_ref = None if v_is_k0 else next(it)
    cq_ref = ck_ref = bm_ref = None
    if has_bias:
        cq_ref, ck_ref = next(it), next(it)
    if has_bm:
        bm_ref = next(it)
    o_ref = next(it)
    qs_refs = [next(it) for _ in dks]
    s_ref, p_ref, m_ref, l_ref, a_ref, acc_ref, madd_ref = (next(it) for _ in range(7))
    cqb_ref = next(it) if has_bias else None
    rc = min(ROW_CHUNK, tq)

    qi = pl.program_id(2)
    ks = pl.program_id(3)
    kt = _kv_lo(qi, tq, tk, window) + ks

    @pl.when(ks == 0)
    def _init():
        for qr, qs, dk in zip(q_refs, qs_refs, dks):
            for h in range(H):
                qs[h * tq:(h + 1) * tq, :] = qr[0, :, h * dk:(h + 1) * dk].astype(BF16)
        if has_bias:
            for h in range(H):
                cqb_ref[h * tq:(h + 1) * tq, :] = jnp.broadcast_to(cq_ref[0, 0, :, h:h + 1] * LOG2E, (tq, LANES))
        m_ref[...] = jnp.full(m_ref.shape, NEG, F32)
        l_ref[...] = jnp.zeros(l_ref.shape, F32)
        acc_ref[...] = jnp.zeros(acc_ref.shape, F32)

    def build_mask():
        qpos = qi * tq + lax.broadcasted_iota(jnp.int32, (tq, tk), 0)
        kpos = kt * tk + lax.broadcasted_iota(jnp.int32, (tq, tk), 1)
        mask = kpos <= qpos
        if window:
            mask = mask & ((qpos - kpos) < window)
        if has_bm:
            bm = bm_ref[0].astype(BF16)
            nc = bm.shape[1]
            cidx = lax.broadcasted_iota(jnp.int32, (nc, tk), 0)
            kk = kt * tk + lax.broadcasted_iota(jnp.int32, (nc, tk), 1)
            expand = jnp.where(cidx == 2 * lax.shift_right_logical(kk, SEL_SHIFT), 1.0, 0.0).astype(BF16)
            mask = mask & (jnp.dot(bm, expand, preferred_element_type=F32) > 0.5)
        madd_ref[...] = jnp.where(mask, 0.0, NEG)

    def softmax_pass(masked):
        for h in range(H):
            if has_bias:
                ck_row = ck_ref[0, 0, h:h + 1, :] * LOG2E
            for r in range(tq // rc):
                pr = slice(r * rc, (r + 1) * rc)
                rows = slice(h * tq + r * rc, h * tq + (r + 1) * rc)
                t = s_ref[rows, :] * c2
                if has_bias:
                    t = (t + jnp.tile(cqb_ref[rows, :], (1, tk // LANES))) - ck_row
                if masked:
                    t = t + madd_ref[pr, :]
                s_ref[rows, :] = t
                m_prev = m_ref[rows, :]
                m_new = jnp.maximum(m_prev, jnp.max(t, axis=1, keepdims=True))
                a_ref[rows, :] = jnp.exp2(m_prev - m_new)
                m_ref[rows, :] = m_new
        for h in range(H):
            for r in range(tq // rc):
                rows = slice(h * tq + r * rc, h * tq + (r + 1) * rc)
                p = jnp.exp2(s_ref[rows, :] - jnp.tile(m_ref[rows, :], (1, tk // LANES)))
                l_ref[rows, :] = a_ref[rows, :] * l_ref[rows, :] + jnp.sum(p, axis=1, keepdims=True)
                p_ref[rows, :] = p.astype(BF16)

    @pl.when(kt <= _kv_hi(qi, tq, tk))
    def _step():
        kbs = [kr[0].astype(BF16) for kr in k_refs]
        s_full = None
        for qs, kb in zip(qs_refs, kbs):
            d = _dot_nt(qs[...], kb)
            s_full = d if s_full is None else s_full + d
        s_ref[...] = s_full
        if has_bm:
            build_mask()
            softmax_pass(True)
        else:
            visible = kt * tk + tk - 1 <= qi * tq
            if window:
                visible = visible & (qi * tq + tq - 1 - kt * tk < window)

            @pl.when(visible)
            def _():
                softmax_pass(False)

            @pl.when(jnp.logical_not(visible))
            def _():
                build_mask()
                softmax_pass(True)

        vb = kbs[0] if v_is_k0 else v_ref[0].astype(BF16)
        pv = jnp.dot(p_ref[...], vb, preferred_element_type=F32)
        acc_ref[...] = acc_ref[...] * jnp.tile(a_ref[...], (1, dv // LANES)) + pv

    @pl.when(ks == nks - 1)
    def _fin():
        for h in range(H):
            rows = slice(h * tq, (h + 1) * tq)
            ok = m_ref[rows, :] > 0.5 * NEG
            inv = jnp.where(ok, 1.0 / jnp.where(ok, l_ref[rows, :], 1.0), 0.0)
            o_ref[0, :, h * dv:(h + 1) * dv] = (acc_ref[rows, :] * jnp.tile(inv, (1, dv // LANES))).astype(o_ref.dtype)


def flash(qs, ks, v, *, H, G=1, dks, dv, scale, q_cols=None, k_cols=None, v_col=0, window=0,
          cq=None, ck=None, bm=None, tq=128, tk=512, out_dtype=F32):
    B, S = qs[0].shape[:2]
    n = len(dks)
    q_cols = q_cols or [0] * n
    k_cols = k_cols or [0] * n
    tq = min(tq, S)
    tk = min(tk, S)
    assert tk % LANES == 0 and dv % LANES == 0
    nq = S // tq
    nks = max(_kv_hi(i, tq, tk) - _kv_lo(i, tq, tk, window) + 1 for i in range(nq))

    def kblk(qi, s):
        return jnp.minimum(_kv_lo(qi, tq, tk, window) + s, _kv_hi(qi, tq, tk))

    in_specs, args = [], []
    for a, dk, c in zip(qs, dks, q_cols):
        in_specs.append(pl.BlockSpec((1, tq, H * dk), lambda b, g, i, s, c=c: (b, i, c + g)))
        args.append(a)
    for a, dk, c in zip(ks, dks, k_cols):
        in_specs.append(pl.BlockSpec((1, tk, dk), lambda b, g, i, s, c=c: (b, kblk(i, s), c + g)))
        args.append(a)
    if v is not None:
        in_specs.append(pl.BlockSpec((1, tk, dv), lambda b, g, i, s: (b, kblk(i, s), v_col + g)))
        args.append(v)
    if cq is not None:
        in_specs.append(pl.BlockSpec((1, 1, tq, H), lambda b, g, i, s: (b, g, i, 0)))
        in_specs.append(pl.BlockSpec((1, 1, H, tk), lambda b, g, i, s: (b, g, 0, kblk(i, s))))
        args += [cq, ck]
    if bm is not None:
        in_specs.append(pl.BlockSpec((1, tq, bm.shape[-1]), lambda b, g, i, s: (b, i, 0)))
        args.append(bm)
    R = H * tq
    scratch = [pltpu.VMEM((R, dk), BF16) for dk in dks]
    scratch += [pltpu.VMEM((R, tk), F32), pltpu.VMEM((R, tk), BF16),
                pltpu.VMEM((R, LANES), F32), pltpu.VMEM((R, LANES), F32), pltpu.VMEM((R, LANES), F32),
                pltpu.VMEM((R, dv), F32), pltpu.VMEM((tq, tk), F32)]
    if cq is not None:
        scratch.append(pltpu.VMEM((R, LANES), F32))
    return pl.pallas_call(
        functools.partial(_flash_body, H=H, tq=tq, tk=tk, dks=tuple(dks), dv=dv, c2=scale * LOG2E,
                          window=window, has_bias=cq is not None, has_bm=bm is not None,
                          v_is_k0=v is None, nks=nks),
        grid=(B, G, nq, nks),
        in_specs=in_specs,
        out_specs=pl.BlockSpec((1, tq, H * dv), lambda b, g, i, s: (b, i, g)),
        out_shape=jax.ShapeDtypeStruct((B, S, G * H * dv), out_dtype),
        scratch_shapes=scratch,
        compiler_params=_cparams("parallel", "parallel", "parallel", "arbitrary"),
        name="flash",
    )(*args)


def _cmp_topk_body(q_ref, km_ref, vm_ref, oc_ref, bm_ref, p_ref, *, H, tq, ncp, pos0, tvalid, stacked):
    qi = pl.program_id(1)
    if stacked:
        qs = q_ref[0].astype(BF16)
    else:
        qs = jnp.concatenate([q_ref[0, :, h * NSA_DH:(h + 1) * NSA_DH] for h in range(H)],
                             axis=0).astype(BF16)
    s_full = _dot_nt(qs, km_ref[0].astype(BF16))
    r = lax.broadcasted_iota(jnp.int32, (tq, ncp), 0)
    c = lax.broadcasted_iota(jnp.int32, (tq, ncp), 1)
    if tvalid is None:
        pos = pos0 + qi * tq + r
    else:
        pos = pos0 + jnp.minimum(r, tvalid - 1)
    cmask = (c + 1) * NSA_CMP - 1 <= pos
    imp = jnp.zeros((tq, ncp), F32)
    for h in range(H):
        rows = slice(h * tq, (h + 1) * tq)
        s = jnp.where(cmask, s_full[rows] * NSA_SCALE, NEG)
        m = jnp.max(s, axis=-1, keepdims=True)
        p = jnp.where(cmask, jnp.exp(s - m), 0.0)
        p = p / jnp.maximum(jnp.sum(p, axis=-1, keepdims=True), 1e-30)
        imp = imp + p
        p_ref[rows] = p.astype(BF16)
    o = jnp.dot(p_ref[...], vm_ref[0].astype(BF16), preferred_element_type=F32)
    if stacked:
        oc_ref[0] = o.astype(oc_ref.dtype)
    else:
        for h in range(H):
            oc_ref[0, :, h * NSA_DH:(h + 1) * NSA_DH] = o[h * tq:(h + 1) * tq].astype(oc_ref.dtype)
    nxt = pltpu.roll(imp, ncp - 1, axis=1)
    prv = pltpu.roll(imp, 1, axis=1)
    ps = imp + jnp.where((c & 1) == 0, nxt, prv)
    blk = lax.shift_right_logical(c, 1)
    avail = blk * NSA_SEL <= pos
    cur = lax.shift_right_logical(pos, SEL_SHIFT)
    forced = (blk == 0) | (blk == cur) | (blk == cur - 1)
    sc = jnp.where(avail, ps + jnp.where(forced, NSA_FORCE, 0.0), NEG)
    rank = jnp.zeros((tq, ncp), F32)
    for i in range(ncp // 2):
        col = sc[:, 2 * i:2 * i + 1]
        beats = (col > sc) | ((col == sc) & (blk > i))
        rank = rank + jnp.where(beats, 1.0, 0.0)
    bm_ref[0] = jnp.where(avail & (rank < NSA_TOPK), 1.0, 0.0)


def cmp_topk(q, kmean, vmean, *, tq, pos0, tvalid=None, stacked=False):
    B = q.shape[0]
    ncp = kmean.shape[1]
    H = NSA_H
    if stacked:
        nq = 1
        q_spec = pl.BlockSpec((1, H * tq, NSA_DH), lambda b, i: (b, 0, 0))
        o_spec = pl.BlockSpec((1, H * tq, NSA_DH), lambda b, i: (b, 0, 0))
        o_shape = jax.ShapeDtypeStruct((B, H * tq, NSA_DH), F32)
    else:
        nq = q.shape[1] // tq
        q_spec = pl.BlockSpec((1, tq, H * NSA_DH), lambda b, i: (b, i, 0))
        o_spec = pl.BlockSpec((1, tq, H * NSA_DH), lambda b, i: (b, i, 0))
        o_shape = jax.ShapeDtypeStruct((B, nq * tq, H * NSA_DH), F32)
    return pl.pallas_call(
        functools.partial(_cmp_topk_body, H=H, tq=tq, ncp=ncp, pos0=pos0, tvalid=tvalid, stacked=stacked),
        grid=(B, nq),
        in_specs=[q_spec,
                  pl.BlockSpec((1, ncp, NSA_DH), lambda b, i: (b, 0, 0)),
                  pl.BlockSpec((1, ncp, NSA_DH), lambda b, i: (b, 0, 0))],
        out_specs=[o_spec, pl.BlockSpec((1, tq, ncp), lambda b, i: (b, i, 0))],
        out_shape=[o_shape, jax.ShapeDtypeStruct((B, nq * tq, ncp), F32)],
        scratch_shapes=[pltpu.VMEM((H * tq, ncp), BF16)],
        compiler_params=_cparams("parallel", "arbitrary"),
        name="cmp_topk",
    )(q, kmean, vmean)


def _block_mean_body(x_ref, o_ref, *, rows):
    x = x_ref[0].astype(F32)
    o_ref[0] = jnp.sum(x.reshape(rows // NSA_CMP, NSA_CMP, x.shape[-1]), axis=1) * (1.0 / NSA_CMP)


def block_mean(x, *, tr=1024):
    B, S, d = x.shape
    tr = min(tr, S)
    assert S % tr == 0 and (tr // NSA_CMP) % SUBLANES == 0
    return pl.pallas_call(
        functools.partial(_block_mean_body, rows=tr),
        grid=(B, S // tr),
        in_specs=[pl.BlockSpec((1, tr, d), lambda b, i: (b, i, 0))],
        out_specs=pl.BlockSpec((1, tr // NSA_CMP, d), lambda b, i: (b, i, 0)),
        out_shape=jax.ShapeDtypeStruct((B, S // NSA_CMP, d), F32),
        compiler_params=_cparams("parallel", "arbitrary"),
        name="block_mean",
    )(x)


def _pool_mean_body(pt_ref, *refs, pp):
    o_ref = refs[-1]
    per = PAGE // NSA_CMP
    for i in range(pp):
        x = refs[i][...].astype(F32)
        o_ref[0, i * per:(i + 1) * per, :] = jnp.sum(x.reshape(per, NSA_CMP, x.shape[-1]), axis=1) * (1.0 / NSA_CMP)


def pool_block_mean(pool, layer, pt_flat, db, npages, *, pp=32):
    d = pool.shape[-1]
    pp = min(pp, npages)
    assert npages % pp == 0
    per = PAGE // NSA_CMP
    in_specs = [pl.BlockSpec((None, None, PAGE, d),
                             lambda b, c, pt, i=i: (layer, pt[b * npages + c * pp + i], 0, 0))
                for i in range(pp)]
    return pl.pallas_call(
        functools.partial(_pool_mean_body, pp=pp),
        grid_spec=pltpu.PrefetchScalarGridSpec(
            num_scalar_prefetch=1, grid=(db, npages // pp), in_specs=in_specs,
            out_specs=pl.BlockSpec((1, pp * per, d), lambda b, c, pt: (b, c, 0))),
        out_shape=jax.ShapeDtypeStruct((db, npages * per, d), F32),
        compiler_params=_cparams("parallel", "arbitrary"),
        name="pool_block_mean",
    )(pt_flat, *([pool] * pp))


def _cumsum_body(x_ref, c_ref, ct_ref, carry_ref, *, blk):
    x = x_ref[0].astype(F32)
    nh = x.shape[-1]

    @pl.when(pl.program_id(1) == 0)
    def _():
        carry_ref[...] = jnp.zeros(carry_ref.shape, F32)

    r = lax.broadcasted_iota(jnp.int32, (blk, blk), 0)
    cc = lax.broadcasted_iota(jnp.int32, (blk, blk), 1)
    tri = jnp.where(cc <= r, 1.0, 0.0)
    c = jnp.dot(tri, x, preferred_element_type=F32, precision=lax.Precision.HIGHEST) + carry_ref[...]
    carry_ref[...] = c[blk - 1:blk, :]
    c_ref[0] = c
    eye = jnp.where(lax.broadcasted_iota(jnp.int32, (nh, nh), 0) == lax.broadcasted_iota(jnp.int32, (nh, nh), 1),
                    1.0, 0.0)
    ct_ref[0] = lax.dot_general(eye, c, (((1,), (1,)), ((), ())), preferred_element_type=F32,
                                precision=lax.Precision.HIGHEST)


def cumsum_seq(x):
    B, S, nh = x.shape
    blk = min(PAGE, S)
    return pl.pallas_call(
        functools.partial(_cumsum_body, blk=blk),
        grid=(B, S // blk),
        in_specs=[pl.BlockSpec((1, blk, nh), lambda b, i: (b, i, 0))],
        out_specs=[pl.BlockSpec((1, blk, nh), lambda b, i: (b, i, 0)),
                   pl.BlockSpec((1, nh, blk), lambda b, i: (b, 0, i))],
        out_shape=[jax.ShapeDtypeStruct((B, S, nh), F32), jax.ShapeDtypeStruct((B, nh, S), F32)],
        scratch_shapes=[pltpu.VMEM((1, nh), F32)],
        compiler_params=_cparams("parallel", "arbitrary"),
        name="cumsum_seq",
    )(x)


def _cumsum_pages_body(pt_ref, *refs, npages):
    o_ref = refs[-1]
    nh = o_ref.shape[1]
    lane = lax.broadcasted_iota(jnp.int32, (nh, PAGE), 1)
    carry = jnp.zeros((nh, 1), F32)
    for j in range(npages):
        x = refs[j][...].astype(F32)
        sh = 1
        while sh < PAGE:
            x = x + jnp.where(lane >= sh, pltpu.roll(x, sh, axis=1), 0.0)
            sh *= 2
        o_ref[0, :, j * PAGE:(j + 1) * PAGE] = x + carry
        carry = carry + x[:, PAGE - 1:PAGE]


def cumsum_pages(pool_t, layer, pt_flat, db, npages):
    nh = pool_t.shape[2]
    in_specs = [pl.BlockSpec((None, None, nh, PAGE), lambda b, pt, j=j: (layer, pt[b * npages + j], 0, 0))
                for j in range(npages)]
    return pl.pallas_call(
        functools.partial(_cumsum_pages_body, npages=npages),
        grid_spec=pltpu.PrefetchScalarGridSpec(
            num_scalar_prefetch=1, grid=(db,), in_specs=in_specs,
            out_specs=pl.BlockSpec((1, nh, npages * PAGE), lambda b, pt: (b, 0, 0))),
        out_shape=jax.ShapeDtypeStruct((db, nh, npages * PAGE), F32),
        compiler_params=_cparams("parallel"),
        name="cumsum_pages",
    )(pt_flat, *([pool_t] * npages))


def _decode_body(pt_ref, *refs, H, G, dks, ktr, pp, dv, c2, window, has_bias, has_bm, v_is_k0,
                 nchunks, kpos0, qpos0, tvalid):
    nparts = len(dks)
    it = iter(refs)
    q_refs = [next(it) for _ in range(nparts)]
    kp_refs = [[next(it) for _ in range(pp)] for _ in range(nparts)]
    vp_refs = kp_refs[0] if v_is_k0 else [next(it) for _ in range(pp)]
    kn_refs = [next(it) for _ in range(nparts)]
    vn_ref = kn_refs[0] if v_is_k0 else next(it)
    cq_ref = ck_ref = ckn_ref = bm_ref = None
    if has_bias:
        cq_ref, ck_ref, ckn_ref = next(it), next(it), next(it)
    if has_bm:
        bm_ref = next(it)
    o_ref = next(it)
    m_ref, l_ref, acc_ref = next(it), next(it), next(it)
    R = H * TPAD
    c = pl.program_id(1)

    @pl.when(c == 0)
    def _init():
        m_ref[...] = jnp.full(m_ref.shape, NEG, F32)
        l_ref[...] = jnp.zeros(l_ref.shape, F32)
        acc_ref[...] = jnp.zeros(acc_ref.shape, F32)

    def page_rows(ref, g):
        return ref[...] if G == 1 else ref[pl.ds(g, PAGE, stride=G), :]

    for g in range(G):
        qbs = [qr[0, g].astype(BF16) for qr in q_refs]

        def update(s, kpos, ck, key_ok, causal, pv, g=g):
            K = s.shape[-1]
            t3 = (s * c2).reshape(H, TPAD, K)
            if has_bias:
                t3 = t3 + ((cq_ref[0, g] * LOG2E).reshape(H, TPAD, 1) - (ck * LOG2E)[:, None, :])
            mask = key_ok
            if causal or window or has_bm:
                tok = jnp.minimum(lax.broadcasted_iota(jnp.int32, (TPAD, K), 0), tvalid - 1)
                qpos = qpos0 + tok
                if causal:
                    cm = kpos <= qpos
                    mask = cm if mask is None else mask & cm
                if window:
                    wm = (qpos - kpos) < window
                    mask = wm if mask is None else mask & wm
                if has_bm:
                    bm = bm_ref[0].astype(BF16)
                    nc = bm.shape[1]
                    cidx = lax.broadcasted_iota(jnp.int32, (nc, K), 0)
                    kk = jnp.broadcast_to(kpos[0:1, :], (nc, K))
                    expand = jnp.where(cidx == 2 * lax.shift_right_logical(kk, SEL_SHIFT), 1.0, 0.0).astype(BF16)
                    sm = jnp.dot(bm, expand, preferred_element_type=F32) > 0.5
                    mask = sm if mask is None else mask & sm
            if mask is not None:
                t3 = jnp.where(mask[None], t3, NEG)
            t = t3.reshape(R, K)
            m_prev = m_ref[g]
            m_new = jnp.maximum(m_prev, jnp.max(t, axis=-1, keepdims=True))
            alpha = jnp.exp2(m_prev - m_new)
            p = jnp.exp2(t - m_new)
            l_ref[g] = alpha * l_ref[g] + jnp.sum(p, axis=-1, keepdims=True)
            m_ref[g] = m_new
            acc_ref[g] = acc_ref[g] * alpha + pv(p.astype(BF16))

        kb0 = []
        s_list = []
        for j in range(pp):
            s = None
            for i in range(nparts):
                if ktr[i]:
                    d = jnp.dot(qbs[i], kp_refs[i][j][...].astype(BF16), preferred_element_type=F32)
                else:
                    kb = page_rows(kp_refs[i][j], g).astype(BF16)
                    if i == 0:
                        kb0.append(kb)
                    d = _dot_nt(qbs[i], kb)
                s = d if s is None else s + d
            s_list.append(s)
        s_all = jnp.concatenate(s_list, axis=1) if pp > 1 else s_list[0]
        K = pp * PAGE
        kpos = kpos0 + c * K + lax.broadcasted_iota(jnp.int32, (TPAD, K), 1)

        def pv_past(pb, g=g, kb0=kb0):
            out = None
            for j in range(pp):
                vb = kb0[j] if v_is_k0 else page_rows(vp_refs[j], g).astype(BF16)
                d = jnp.dot(pb[:, j * PAGE:(j + 1) * PAGE], vb, preferred_element_type=F32)
                out = d if out is None else out + d
            return out

        update(s_all, kpos, ck_ref[0, g] if has_bias else None, None, False, pv_past)

        @pl.when(c == nchunks - 1)
        def _new_and_fin(g=g, qbs=qbs, update=update):
            knb = [kn[0, :, g * dk:(g + 1) * dk].astype(BF16) for kn, dk in zip(kn_refs, dks)]
            s = None
            for i in range(nparts):
                d = _dot_nt(qbs[i], knb[i])
                s = d if s is None else s + d
            j = lax.broadcasted_iota(jnp.int32, (TPAD, TPAD), 1)
            vnb = knb[0] if v_is_k0 else vn_ref[0, :, g * dv:(g + 1) * dv].astype(BF16)
            update(s, qpos0 + j, ckn_ref[0, g] if has_bias else None, j < tvalid, True,
                   lambda pb: jnp.dot(pb, vnb, preferred_element_type=F32))
            ok = m_ref[g] > 0.5 * NEG
            o = jnp.where(ok, acc_ref[g] / jnp.where(ok, l_ref[g], 1.0), 0.0)
            o_ref[0, g] = o.astype(o_ref.dtype)


def decode(qs, pools, vpool, layer, pt_flat, npages, knews, vnew, *, H, G=1, dks, dv, scale,
           qpos0, kpos0=0, tvalid, ktr=None, window=0, cq=None, ck=None, cknew=None, bm=None, pp=32):
    DB = qs[0].shape[0]
    nparts = len(dks)
    ktr = ktr or (False,) * nparts
    pp = min(pp, npages)
    assert npages % pp == 0
    nchunks = npages // pp
    R = H * TPAD
    in_specs, args = [], []
    for a, dk in zip(qs, dks):
        in_specs.append(pl.BlockSpec((1, G, R, dk), lambda b, c, pt: (b, 0, 0, 0)))
        args.append(a)

    def page_specs(shape):
        return [pl.BlockSpec((None, None) + shape,
                             lambda b, c, pt, j=j: (layer, pt[b * npages + c * pp + j], 0, 0))
                for j in range(pp)]

    for a, dk, tr in zip(pools, dks, ktr):
        in_specs += page_specs((dk, PAGE) if tr else (PAGE * G, dk))
        args += [a] * pp
    if vpool is not None:
        in_specs += page_specs((PAGE * G, dv))
        args += [vpool] * pp
    for a, dk in zip(knews, dks):
        in_specs.append(pl.BlockSpec((1, TPAD, G * dk), lambda b, c, pt: (b, 0, 0)))
        args.append(a)
    if vnew is not None:
        in_specs.append(pl.BlockSpec((1, TPAD, G * dv), lambda b, c, pt: (b, 0, 0)))
        args.append(vnew)
    if cq is not None:
        in_specs.append(pl.BlockSpec((1, G, R, 1), lambda b, c, pt: (b, 0, 0, 0)))
        in_specs.append(pl.BlockSpec((1, G, H, pp * PAGE), lambda b, c, pt: (b, 0, 0, c)))
        in_specs.append(pl.BlockSpec((1, G, H, TPAD), lambda b, c, pt: (b, 0, 0, 0)))
        args += [cq, ck, cknew]
    if bm is not None:
        in_specs.append(pl.BlockSpec((1, TPAD, bm.shape[-1]), lambda b, c, pt: (b, 0, 0)))
        args.append(bm)
    return pl.pallas_call(
        functools.partial(_decode_body, H=H, G=G, dks=tuple(dks), ktr=tuple(ktr), pp=pp, dv=dv,
                          c2=scale * LOG2E, window=window, has_bias=cq is not None, has_bm=bm is not None,
                          v_is_k0=vpool is None, nchunks=nchunks, kpos0=kpos0, qpos0=qpos0, tvalid=tvalid),
        grid_spec=pltpu.PrefetchScalarGridSpec(
            num_scalar_prefetch=1, grid=(DB, nchunks), in_specs=in_specs,
            out_specs=pl.BlockSpec((1, G, R, dv), lambda b, c, pt: (b, 0, 0, 0)),
            scratch_shapes=[pltpu.VMEM((G, R, 1), F32), pltpu.VMEM((G, R, 1), F32),
                            pltpu.VMEM((G, R, dv), F32)]),
        out_shape=jax.ShapeDtypeStruct((DB, G, R, dv), F32),
        compiler_params=_cparams("parallel", "arbitrary"),
        name="decode",
    )(pt_flat, *args)


def _nsa_mix_body(oc_ref, os_ref, ow_ref, gt_ref, g_ref, o_ref):
    gt = gt_ref[...]
    for h in range(NSA_H):
        cols = slice(h * NSA_DH, (h + 1) * NSA_DH)
        o = (gt[:, h:h + 1] * oc_ref[:, cols] + gt[:, NSA_H + h:NSA_H + h + 1] * os_ref[:, cols]
             + gt[:, 2 * NSA_H + h:2 * NSA_H + h + 1] * ow_ref[:, cols])
        g = g_ref[:, cols]
        o_ref[:, cols] = (o * (g * jax.nn.sigmoid(g))).astype(o_ref.dtype)


def nsa_mix(oc, osel, ow, gates, proj, g_col, *, tm=256):
    M, W = oc.shape
    tm = min(tm, M)
    assert M % tm == 0
    spec = pl.BlockSpec((tm, W), lambda i: (i, 0))
    return pl.pallas_call(
        _nsa_mix_body,
        grid=(M // tm,),
        in_specs=[spec, spec, spec, pl.BlockSpec((tm, gates.shape[1]), lambda i: (i, 0)),
                  pl.BlockSpec((tm, W), lambda i: (i, g_col))],
        out_specs=spec,
        out_shape=jax.ShapeDtypeStruct((M, W), BF16),
        compiler_params=_cparams("parallel"),
        name="nsa_mix",
    )(oc, osel, ow, gates, proj)


def _shift_append_body(buf_ref, new_ref, o_ref, *, wb, t):
    o_ref[:, 0:wb - t, :] = buf_ref[:, t:wb, :]
    o_ref[:, wb - t:wb, :] = new_ref[:, 0:t, :]


def shift_append(state, layer, new, t, *, tb=8):
    _, DB, wb, d = state.shape
    tb = min(tb, DB)
    assert DB % tb == 0
    return pl.pallas_call(
        functools.partial(_shift_append_body, wb=wb, t=t),
        grid=(DB // tb,),
        in_specs=[pl.BlockSpec((None, tb, wb, d), lambda i: (layer, i, 0, 0)),
                  pl.BlockSpec((tb, TPAD, d), lambda i: (i, 0, 0))],
        out_specs=pl.BlockSpec((tb, wb, d), lambda i: (i, 0, 0)),
        out_shape=jax.ShapeDtypeStruct((DB, wb, d), state.dtype),
        compiler_params=_cparams("parallel"),
        name="shift_append",
    )(state, new)


def _rope(x, pos):
    half = x.shape[-1] // 2
    inv = ROPE_THETA ** (-jnp.arange(half, dtype=F32) / half)
    ang = pos.astype(F32)[:, None] * inv[None, :]
    cos = jnp.cos(ang)[:, None, :]
    sin = jnp.sin(ang)[:, None, :]
    x1, x2 = x[..., :half], x[..., half:]
    return jnp.concatenate([x1 * cos - x2 * sin, x1 * sin + x2 * cos], axis=-1)


def _pad_rows(x, n, axis):
    pad = [(0, 0)] * x.ndim
    pad[axis] = (0, n - x.shape[axis])
    return jnp.pad(x, pad)


def _stack_rows(x, H, d):
    DB, T, _ = x.shape
    x = x.reshape(DB, T, H, d).transpose(0, 2, 1, 3)
    return _pad_rows(x, TPAD, 2).reshape(DB, 1, H * TPAD, d)


def _unstack_rows(o, H, T):
    DB, _, d = o.shape
    return o.reshape(DB, H, TPAD, d)[:, :, :T].transpose(0, 2, 1, 3).reshape(DB, T, H * d)


def _even_weights(w_in, w_uq, w_uk, w_uv, w_out):
    D = w_in.shape[0]
    o = np.cumsum((0, MLA_QL, MLA_KVL, MLA_ROPE, MLA_H * MLA_V, FOX_H * FOX_DH, FOX_KVH * FOX_DH,
                   FOX_KVH * FOX_DH, FOX_H, FOX_H * FOX_DH))
    seg = lambda i: w_in[:, o[i]:o[i + 1]]
    order = [seg(4), seg(3), seg(8), seg(0), seg(1), seg(5), seg(6), seg(2), seg(7)]
    n = sum(s.shape[1] for s in order)
    w = jnp.concatenate(order + [jnp.zeros((D, _round_up(n, 512) - n), w_in.dtype)], axis=1).astype(BF16)
    uq = jnp.concatenate([w_uq[:, :, :MLA_NOPE].reshape(MLA_QL, -1), w_uq[:, :, MLA_NOPE:].reshape(MLA_QL, -1)],
                         axis=1).astype(BF16)
    uk = w_uk.transpose(1, 2, 0).astype(BF16)
    uv = w_uv.transpose(1, 0, 2).astype(BF16)
    return w, uq, uk, uv, w_out.astype(BF16)


EV_COL = dict(fq=0, gm=1024, gf=2048, cq=3072, ckv=3584, fk=3840, fv=4096, kr=4352, fl=4416)


def _odd_weights(w_in, gate_bias, w_out):
    D = w_in.shape[0]
    W = NSA_H * NSA_DH
    q, kv, gl, g = w_in[:, :W], w_in[:, W:W + 6 * NSA_DH], w_in[:, W + 6 * NSA_DH:W + 6 * NSA_DH + 3 * NSA_H], \
        w_in[:, W + 6 * NSA_DH + 3 * NSA_H:]
    gl = gl.reshape(D, NSA_H, 3).transpose(0, 2, 1).reshape(D, 3 * NSA_H)
    n = 2 * W + 6 * NSA_DH + 3 * NSA_H
    w = jnp.concatenate([q, g, kv, gl, jnp.zeros((D, _round_up(n, 512) - n), w_in.dtype)], axis=1).astype(BF16)
    gb = gate_bias.reshape(NSA_H, 3).T.reshape(3 * NSA_H)
    return w, gb, w_out.astype(BF16)


def kernel(x_prompt, x_sample, cache_mla_ckv, cache_mla_krope, cache_fox_k, cache_fox_v, cache_fox_logf, cache_nsa_cmp_k, cache_nsa_cmp_v, cache_nsa_sel_k, cache_nsa_sel_v, state_nsa_win_k, state_nsa_win_v, page_table, c_prompt, c_sample, ada_w, ada_b, norm_g, final_g, ev_w_in, mla_q_norm, mla_kv_norm, mla_w_uq, mla_w_uk, mla_w_uv, fox_f_bias, ev_w_out, od_w_in, nsa_gate_bias, od_w_out):
    B, S, D = x_prompt.shape
    DB, T, _ = x_sample.shape
    depth = ada_w.shape[0]
    npages = page_table.shape[1]
    P = npages * PAGE
    MS = DB * T
    pos_p = jnp.arange(S)
    pos_s = P + jnp.arange(T)
    pt_flat = page_table.reshape(-1).astype(jnp.int32)

    c_all = jnp.concatenate([c_prompt, c_sample], axis=0)
    mc = _round_up(B + DB, SUBLANES)
    c_all = _pad_rows(c_all, mc, 0)[None]
    fox_pool_k = cache_fox_k.reshape(cache_fox_k.shape[:2] + (PAGE * FOX_KVH, FOX_DH))
    fox_pool_v = cache_fox_v.reshape(cache_fox_v.shape[:2] + (PAGE * FOX_KVH, FOX_DH))
    krope_pool_t = jnp.swapaxes(cache_mla_krope, 2, 3)
    logf_pool_t = jnp.swapaxes(cache_fox_logf, 2, 3)

    xp = x_prompt
    xs = x_sample.reshape(1, MS, D)
    outs = {k: [] for k in ('ckv_p', 'ckv_s', 'kr_p', 'kr_s', 'fk_p', 'fk_s', 'fv_p', 'fv_s', 'lf_p', 'lf_s',
                            'kc_p', 'kc_s', 'vc_p', 'vc_s', 'ks_p', 'ks_s', 'vs_p', 'vs_s',
                            'wk_p', 'wk_s', 'wv_p', 'wv_s')}

    for l in range(depth):
        mod = mm(c_all, ada_w, w_idx=l, bias=ada_b[l][None])[0]
        shift, scale, gate = mod[:, :D], mod[:, D:2 * D], mod[:, 2 * D:]
        rep = lambda a: jnp.repeat(a[B:B + DB], T, axis=0)[None]
        hp = norm(xp, norm_g[l], scale[:B, None], shift[:B, None], out_dtype=BF16)
        hs = norm(xs, norm_g[l], rep(scale), rep(shift), out_dtype=BF16)
        gate_p, gate_s = gate[:B, None], rep(gate)

        if l % 2 == 0:
            e = l // 2
            w_in, w_uq, w_uk, w_uv, w_out = _even_weights(ev_w_in[e], mla_w_uq[e], mla_w_uk[e], mla_w_uv[e],
                                                          ev_w_out[e])
            C = EV_COL

            def project(h, pos):
                proj = mm(h, w_in)
                g_, r_, _ = proj.shape
                cqn = norm(proj, mla_q_norm[e], col=C['cq'] // MLA_QL, width=MLA_QL, out_dtype=BF16)
                q = mm(cqn, w_uq)
                q_lat = mm_heads(q.reshape(g_ * r_, -1), w_uk).reshape(g_, r_, -1)
                q_rope = _rope(q[..., MLA_H * MLA_NOPE:].reshape(g_, r_, MLA_H, MLA_ROPE), pos).reshape(g_, r_, -1)
                ckv = norm(proj, mla_kv_norm[e], col=C['ckv'] // MLA_KVL, width=MLA_KVL)
                kr = _rope(proj[..., None, C['kr']:C['kr'] + MLA_ROPE], pos)[..., 0, :]
                logf = jax.nn.log_sigmoid(proj[..., C['fl']:C['fl'] + FOX_H] + fox_f_bias[e])
                return proj, q_lat, q_rope, ckv, kr, logf

            def output(proj, o_lat, o_fox, x, gate_):
                g_, r_, _ = proj.shape
                o_mla = mm_heads(o_lat.reshape(g_ * r_, -1), w_uv).reshape(g_, r_, -1)
                gm = proj[..., C['gm']:C['gm'] + MLA_H * MLA_V]
                gf = proj[..., C['gf']:C['gf'] + FOX_H * FOX_DH]
                mix = jnp.concatenate([o_mla * jax.nn.silu(gm), o_fox * jax.nn.silu(gf)], axis=-1).astype(BF16)
                return mm(mix, w_out, res=x, gate=gate_)

            proj, q_lat, q_rope, ckv, kr, logf = project(hp, pos_p)
            o_lat = flash([q_lat, q_rope], [ckv, kr], None, H=MLA_H, dks=(MLA_KVL, MLA_ROPE), dv=MLA_KVL,
                          scale=MLA_SCALE)
            csum, csum_t = cumsum_seq(logf)
            cq = csum.reshape(B, S, FOX_KVH, FOX_G).transpose(0, 2, 1, 3)
            ck = csum_t.reshape(B, FOX_KVH, FOX_G, S)
            kb = C['fk'] // FOX_DH
            vb = C['fv'] // FOX_DH
            o_fox = flash([proj], [proj], proj, H=FOX_G, G=FOX_KVH, dks=(FOX_DH,), dv=FOX_DH, scale=FOX_SCALE,
                          q_cols=[0], k_cols=[kb], v_col=vb, cq=cq, ck=ck)
            fk = proj[..., C['fk']:C['fk'] + FOX_KVH * FOX_DH]
            fv = proj[..., C['fv']:C['fv'] + FOX_KVH * FOX_DH]
            outs['ckv_p'].append(ckv)
            outs['kr_p'].append(kr)
            outs['fk_p'].append(fk.reshape(B, S, FOX_KVH, FOX_DH))
            outs['fv_p'].append(fv.reshape(B, S, FOX_KVH, FOX_DH))
            outs['lf_p'].append(logf)
            xp = output(proj, o_lat, o_fox, xp, gate_p)

            proj, q_lat, q_rope, ckv, kr, logf = project(hs, jnp.tile(pos_s, DB))
            ckv3, kr3 = ckv.reshape(DB, T, -1), kr.reshape(DB, T, -1)
            o_lat = decode([_stack_rows(q_lat.reshape(DB, T, -1), MLA_H, MLA_KVL),
                            _stack_rows(q_rope.reshape(DB, T, -1), MLA_H, MLA_ROPE)],
                           [cache_mla_ckv, krope_pool_t], None, e, pt_flat, npages,
                           [_pad_rows(ckv3, TPAD, 1), _pad_rows(kr3, TPAD, 1)], None,
                           H=MLA_H, dks=(MLA_KVL, MLA_ROPE), ktr=(False, True), dv=MLA_KVL, scale=MLA_SCALE,
                           qpos0=P, tvalid=T)
            o_lat = _unstack_rows(o_lat[:, 0], MLA_H, T).reshape(1, MS, -1)
            cpast_t = cumsum_pages(logf_pool_t, e, pt_flat, DB, npages)
            lf3 = logf.reshape(DB, T, FOX_H)
            c_new = cpast_t[:, :, -1][:, None, :] + jnp.cumsum(lf3, axis=1)
            fq = proj[0, :, :FOX_H * FOX_DH].reshape(DB, T, FOX_KVH, FOX_G, FOX_DH).transpose(0, 2, 3, 1, 4)
            fq = _pad_rows(fq, TPAD, 3).reshape(DB, FOX_KVH, FOX_G * TPAD, FOX_DH)
            cqs = _pad_rows(c_new.reshape(DB, T, FOX_KVH, FOX_G).transpose(0, 2, 3, 1), TPAD, 3)
            fk = proj[0, :, C['fk']:C['fk'] + FOX_KVH * FOX_DH].reshape(DB, T, -1)
            fv = proj[0, :, C['fv']:C['fv'] + FOX_KVH * FOX_DH].reshape(DB, T, -1)
            o_fox = decode([fq], [fox_pool_k], fox_pool_v, e, pt_flat, npages,
                           [_pad_rows(fk, TPAD, 1)], _pad_rows(fv, TPAD, 1),
                           H=FOX_G, G=FOX_KVH, dks=(FOX_DH,), dv=FOX_DH, scale=FOX_SCALE, qpos0=P, tvalid=T,
                           cq=cqs.reshape(DB, FOX_KVH, FOX_G * TPAD, 1),
                           ck=cpast_t.reshape(DB, FOX_KVH, FOX_G, P), cknew=cqs)
            o_fox = o_fox.reshape(DB, FOX_KVH, FOX_G, TPAD, FOX_DH)[:, :, :, :T].transpose(0, 3, 1, 2, 4)
            o_fox = o_fox.reshape(1, MS, FOX_H * FOX_DH)
            outs['ckv_s'].append(ckv3)
            outs['kr_s'].append(kr3)
            outs['fk_s'].append(fk.reshape(DB, T, FOX_KVH, FOX_DH))
            outs['fv_s'].append(fv.reshape(DB, T, FOX_KVH, FOX_DH))
            outs['lf_s'].append(lf3)
            xs = output(proj, o_lat, o_fox, xs, gate_s)
        else:
            od = l // 2
            w_in, gbias, w_out = _odd_weights(od_w_in[od], nsa_gate_bias[od], od_w_out[od])
            W = NSA_H * NSA_DH
            KV0 = 2 * W

            def project(h, pos):
                proj = mm(h, w_in)
                g_, r_, _ = proj.shape
                q = _rope(proj[..., :W].reshape(g_, r_, NSA_H, NSA_DH), pos).reshape(g_, r_, W)
                kv = [proj[..., KV0 + i * NSA_DH:KV0 + (i + 1) * NSA_DH] for i in range(6)]
                kc, vc, ks, vs, kw, vw = kv
                kc, ks, kw = (_rope(k[..., None, :], pos)[..., 0, :] for k in (kc, ks, kw))
                gates = jax.nn.sigmoid(proj[..., KV0 + 6 * NSA_DH:KV0 + 6 * NSA_DH + 3 * NSA_H] + gbias)
                return proj, q, kc, vc, ks, vs, kw, vw, gates

            def output(proj, o_c, o_s, o_w, gates, x, gate_):
                g_, r_, _ = proj.shape
                mix = nsa_mix(o_c.reshape(g_ * r_, W), o_s.reshape(g_ * r_, W), o_w.reshape(g_ * r_, W),
                              gates.reshape(g_ * r_, -1), proj.reshape(g_ * r_, -1), 1)
                return mm(mix.reshape(g_, r_, W), w_out, res=x, gate=gate_)

            proj, q, kc, vc, ks, vs, kw, vw, gates = project(hp, pos_p)
            ncp = _round_up(S // NSA_CMP, LANES)
            kmean = _pad_rows(block_mean(kc, tr=min(1024, S)), ncp, 1)
            vmean = _pad_rows(block_mean(vc, tr=min(1024, S)), ncp, 1)
            tq = min(128, S)
            o_c, bm = cmp_topk(q, kmean, vmean, tq=tq, pos0=0)
            o_s = flash([q], [ks], vs, H=NSA_H, dks=(NSA_DH,), dv=NSA_DH, scale=NSA_SCALE, bm=bm, tq=tq)
            o_w = flash([q], [kw], vw, H=NSA_H, dks=(NSA_DH,), dv=NSA_DH, scale=NSA_SCALE, window=NSA_WIN,
                        tq=tq, tk=tq)
            wbp = min(NSA_WIN, S)
            for n_, v_ in zip(('kc_p', 'vc_p', 'ks_p', 'vs_p', 'wk_p', 'wv_p'),
                              (kc, vc, ks, vs, kw[:, S - wbp:], vw[:, S - wbp:])):
                outs[n_].append(v_)
            xp = output(proj, o_c, o_s, o_w, gates, xp, gate_p)

            proj, q, kc, vc, ks, vs, kw, vw, gates = project(hs, jnp.tile(pos_s, DB))
            r3 = lambda a: a.reshape(DB, T, -1)
            kc, vc, ks, vs, kw, vw = map(r3, (kc, vc, ks, vs, kw, vw))
            qst = _stack_rows(r3(q), NSA_H, NSA_DH)
            ncp = _round_up((P + T + NSA_SEL - 1) // NSA_SEL * 2, LANES)
            kmean = _pad_rows(pool_block_mean(cache_nsa_cmp_k, od, pt_flat, DB, npages), ncp, 1)
            vmean = _pad_rows(pool_block_mean(cache_nsa_cmp_v, od, pt_flat, DB, npages), ncp, 1)
            o_c, bm = cmp_topk(qst[:, 0], kmean, vmean, tq=TPAD, pos0=P, tvalid=T, stacked=True)
            pad8 = lambda a: _pad_rows(a, TPAD, 1)
            o_s = decode([qst], [cache_nsa_sel_k], cache_nsa_sel_v, od, pt_flat, npages, [pad8(ks)], pad8(vs),
                         H=NSA_H, dks=(NSA_DH,), dv=NSA_DH, scale=NSA_SCALE, qpos0=P, tvalid=T, bm=bm)
            wb = state_nsa_win_k.shape[2]
            wpages = wb // PAGE
            wtab = jnp.arange(DB * wpages, dtype=jnp.int32)
            wshape = (state_nsa_win_k.shape[0], DB * wpages, PAGE, NSA_DH)
            o_w = decode([qst], [state_nsa_win_k.reshape(wshape)], state_nsa_win_v.reshape(wshape), od, wtab,
                         wpages, [pad8(kw)], pad8(vw), H=NSA_H, dks=(NSA_DH,), dv=NSA_DH, scale=NSA_SCALE,
                         qpos0=P, kpos0=P - wb, tvalid=T, window=NSA_WIN, pp=wpages)
            un = lambda o: _unstack_rows(o, NSA_H, T).reshape(1, MS, W)
            bk = shift_append(state_nsa_win_k, od, pad8(kw), T)
            bv = shift_append(state_nsa_win_v, od, pad8(vw), T)
            for n_, v_ in zip(('kc_s', 'vc_s', 'ks_s', 'vs_s', 'wk_s', 'wv_s'), (kc, vc, ks, vs, bk, bv)):
                outs[n_].append(v_)
            xs = output(proj, un(o_c), un(o_s[:, 0]), un(o_w[:, 0]), gates, xs, gate_s)

    y_prompt = norm(xp, final_g)
    y_sample = norm(xs, final_g).reshape(DB, T, D)
    st = {k: jnp.stack(v) for k, v in outs.items()}
    return (y_prompt, y_sample, st['ckv_p'], st['ckv_s'], st['kr_p'], st['kr_s'],
            st['fk_p'], st['fk_s'], st['fv_p'], st['fv_s'], st['lf_p'], st['lf_s'],
            st['kc_p'], st['kc_s'], st['vc_p'], st['vc_s'], st['ks_p'], st['ks_s'],
            st['vs_p'], st['vs_s'], st['wk_p'], st['wk_s'], st['wv_p'], st['wv_s'])
```

```python
import functools
import math

import numpy as np
import jax
import jax.numpy as jnp
from jax import lax
from jax.experimental import pallas as pl
from jax.experimental.pallas import tpu as pltpu

F32 = jnp.float32
BF16 = jnp.bfloat16
NEG = -1e30
LOG2E = math.log2(math.e)
VMEM_LIMIT = 48 * 1024 * 1024
LANES = 128
SUBLANES = 8

EPS = 1e-6
ROPE_THETA = 10000.0
PAGE = 128
MLA_H, MLA_NOPE, MLA_ROPE, MLA_V, MLA_QL, MLA_KVL = 8, 128, 64, 128, 512, 256
MLA_SCALE = (MLA_NOPE + MLA_ROPE) ** -0.5
FOX_H, FOX_KVH, FOX_DH = 8, 2, 128
FOX_G = FOX_H // FOX_KVH
FOX_SCALE = FOX_DH ** -0.5
NSA_H, NSA_DH, NSA_CMP, NSA_SEL, NSA_TOPK, NSA_WIN = 16, 128, 32, 64, 16, 512
NSA_FORCE = 1000.0
NSA_SCALE = NSA_DH ** -0.5
SEL_SHIFT = 6
TPAD = SUBLANES
ROW_CHUNK = 16


def _cparams(*sem):
    return pltpu.CompilerParams(dimension_semantics=sem, vmem_limit_bytes=VMEM_LIMIT)


def _dot_nt(a, b):
    return lax.dot_general(a, b, (((1,), (1,)), ((), ())), preferred_element_type=F32)


def _round_up(n, m):
    return -(-n // m) * m


def _mm_body(x_ref, w_ref, *rest, has_bias, has_res):
    o_ref = rest[-1]
    acc = jnp.dot(x_ref[0].astype(BF16), w_ref[...].astype(BF16), preferred_element_type=F32)
    k = 0
    if has_bias:
        acc = acc + rest[k][...]
        k += 1
    if has_res:
        acc = rest[k][0] + rest[k + 1][0] * acc
    o_ref[0] = acc.astype(o_ref.dtype)


def mm(x, w, *, w_idx=None, bias=None, res=None, gate=None, out_dtype=F32, tm=1024, tn=512):
    G, R, K = x.shape
    N = w.shape[-1]
    tm = min(tm, R)
    tn = min(tn, N)
    assert R % tm == 0 and N % tn == 0, (R, tm, N, tn)
    in_specs = [pl.BlockSpec((1, tm, K), lambda g, i, j: (g, i, 0))]
    if w.ndim == 3:
        in_specs.append(pl.BlockSpec((None, K, tn), lambda g, i, j: (w_idx, 0, j)))
    else:
        in_specs.append(pl.BlockSpec((K, tn), lambda g, i, j: (0, j)))
    args = [x, w]
    if bias is not None:
        in_specs.append(pl.BlockSpec((1, tn), lambda g, i, j: (0, j)))
        args.append(bias)
    if res is not None:
        in_specs.append(pl.BlockSpec((1, tm, tn), lambda g, i, j: (g, i, j)))
        args.append(res)
        if gate.shape[1] == 1:
            in_specs.append(pl.BlockSpec((1, 1, tn), lambda g, i, j: (g, 0, j)))
        else:
            in_specs.append(pl.BlockSpec((1, tm, tn), lambda g, i, j: (g, i, j)))
        args.append(gate)
    return pl.pallas_call(
        functools.partial(_mm_body, has_bias=bias is not None, has_res=res is not None),
        grid=(G, R // tm, N // tn),
        in_specs=in_specs,
        out_specs=pl.BlockSpec((1, tm, tn), lambda g, i, j: (g, i, j)),
        out_shape=jax.ShapeDtypeStruct((G, R, N), out_dtype),
        compiler_params=_cparams("parallel", "parallel", "arbitrary"),
        name="mm",
    )(*args)


def _mm_heads_body(x_ref, w_ref, o_ref):
    o_ref[...] = jnp.dot(x_ref[...].astype(BF16), w_ref[0].astype(BF16),
                         preferred_element_type=F32).astype(o_ref.dtype)


def mm_heads(x, w, *, col_off=0, out_dtype=F32, tm=512):
    M = x.shape[0]
    H, dk, dn = w.shape
    tm = min(tm, M)
    assert M % tm == 0
    return pl.pallas_call(
        _mm_heads_body,
        grid=(M // tm, H),
        in_specs=[pl.BlockSpec((tm, dk), lambda i, h: (i, col_off + h)),
                  pl.BlockSpec((1, dk, dn), lambda i, h: (h, 0, 0))],
        out_specs=pl.BlockSpec((tm, dn), lambda i, h: (i, h)),
        out_shape=jax.ShapeDtypeStruct((M, H * dn), out_dtype),
        compiler_params=_cparams("parallel", "arbitrary"),
        name="mm_heads",
    )(x, w)


def _norm_body(x_ref, g_ref, *rest, mod):
    o_ref = rest[-1]
    xf = x_ref[0].astype(F32)
    y = xf * lax.rsqrt(jnp.mean(xf * xf, axis=-1, keepdims=True) + EPS) * g_ref[...]
    if mod:
        y = y * (1.0 + rest[0][0]) + rest[1][0]
    o_ref[0] = y.astype(o_ref.dtype)


def norm(x, g, scale=None, shift=None, *, col=0, width=None, out_dtype=F32, tr=256):
    G, R, _ = x.shape
    width = width or x.shape[-1]
    tr = min(tr, R)
    assert R % tr == 0
    in_specs = [pl.BlockSpec((1, tr, width), lambda a, i: (a, i, col)),
                pl.BlockSpec((1, width), lambda a, i: (0, 0))]
    args = [x, g.reshape(1, width)]
    mod = scale is not None
    if mod:
        for arr in (scale, shift):
            if arr.shape[1] == 1:
                in_specs.append(pl.BlockSpec((1, 1, width), lambda a, i: (a, 0, 0)))
            else:
                in_specs.append(pl.BlockSpec((1, tr, width), lambda a, i: (a, i, 0)))
            args.append(arr)
    return pl.pallas_call(
        functools.partial(_norm_body, mod=mod),
        grid=(G, R // tr),
        in_specs=in_specs,
        out_specs=pl.BlockSpec((1, tr, width), lambda a, i: (a, i, 0)),
        out_shape=jax.ShapeDtypeStruct((G, R, width), out_dtype),
        compiler_params=_cparams("parallel", "arbitrary"),
        name="norm",
    )(*args)


def _kv_hi(qi, tq, tk):
    return (qi * tq + tq - 1) // tk


def _kv_lo(qi, tq, tk, window):
    return max(qi * tq - (window - 1), 0) // tk if window else 0


def _flash_body(tab_ref, *refs, H, tq, tk, dks, dv, c2, window, has_bias, has_bm, v_is_k0, nsteps):
    it = iter(refs)
    q_refs = [next(it) for _ in dks]
    k_refs = [next(it) for _ in dks]
    v_ref = None if v_is_k0 else next(it)
    cq_ref = ck_ref = bm_ref = None
    if has_bias:
        cq_ref, ck_ref = next(it), next(it)
    if has_bm:
        bm_ref = next(it)
    o_ref = next(it)
    qs_refs = [next(it) for _ in dks]
    s_ref, p_ref, m_ref, l_ref, a_ref, acc_ref, madd_ref = (next(it) for _ in range(7))
    cqb_ref = next(it) if has_bias else None
    rc = min(ROW_CHUNK, tq)

    step = pl.program_id(2)
    qi = tab_ref[step]
    kt = tab_ref[nsteps + step]

    @pl.when(tab_ref[2 * nsteps + step] == 1)
    def _init():
        for qr, qs, dk in zip(q_refs, qs_refs, dks):
            for h in range(H):
                qs[h * tq:(h + 1) * tq, :] = qr[0, :, h * dk:(h + 1) * dk].astype(BF16)
        if has_bias:
            for h in range(H):
                cqb_ref[h * tq:(h + 1) * tq, :] = jnp.broadcast_to(cq_ref[0, 0, :, h:h + 1] * LOG2E, (tq, LANES))
        m_ref[...] = jnp.full(m_ref.shape, NEG, F32)
        l_ref[...] = jnp.zeros(l_ref.shape, F32)
        acc_ref[...] = jnp.zeros(acc_ref.shape, F32)

    def build_mask():
        qpos = qi * tq + lax.broadcasted_iota(jnp.int32, (tq, tk), 0)
        kpos = kt * tk + lax.broadcasted_iota(jnp.int32, (tq, tk), 1)
        mask = kpos <= qpos
        if window:
            mask = mask & ((qpos - kpos) < window)
        if has_bm:
            bm = bm_ref[0].astype(BF16)
            nc = bm.shape[1]
            cidx = lax.broadcasted_iota(jnp.int32, (nc, tk), 0)
            kk = kt * tk + lax.broadcasted_iota(jnp.int32, (nc, tk), 1)
            expand = jnp.where(cidx == 2 * lax.shift_right_logical(kk, SEL_SHIFT), 1.0, 0.0).astype(BF16)
            mask = mask & (jnp.dot(bm, expand, preferred_element_type=F32) > 0.5)
        madd_ref[...] = jnp.where(mask, 0.0, NEG)

    def softmax_pass(masked):
        for h in range(H):
            if has_bias:
                ck_row = ck_ref[0, 0, h:h + 1, :] * LOG2E
            for r in range(tq // rc):
                pr = slice(r * rc, (r + 1) * rc)
                rows = slice(h * tq + r * rc, h * tq + (r + 1) * rc)
                t = s_ref[rows, :] * c2
                if has_bias:
                    t = (t + jnp.tile(cqb_ref[rows, :], (1, tk // LANES))) - ck_row
                if masked:
                    t = t + madd_ref[pr, :]
                s_ref[rows, :] = t
                m_prev = m_ref[rows, :]
                m_new = jnp.maximum(m_prev, jnp.max(t, axis=1, keepdims=True))
                a_ref[rows, :] = jnp.exp2(m_prev - m_new)
                m_ref[rows, :] = m_new
        for h in range(H):
            for r in range(tq // rc):
                rows = slice(h * tq + r * rc, h * tq + (r + 1) * rc)
                p = jnp.exp2(s_ref[rows, :] - jnp.tile(m_ref[rows, :], (1, tk // LANES)))
                l_ref[rows, :] = a_ref[rows, :] * l_ref[rows, :] + jnp.sum(p, axis=1, keepdims=True)
                p_ref[rows, :] = p.astype(BF16)

    def _step():
        kbs = [kr[0].astype(BF16) for kr in k_refs]
        s_full = None
        for qs, kb in zip(qs_refs, kbs):
            d = _dot_nt(qs[...], kb)
            s_full = d if s_full is None else s_full + d
        s_ref[...] = s_full
        if has_bm:
            build_mask()
            softmax_pass(True)
        else:
            visible = kt * tk + tk - 1 <= qi * tq
            if window:
                visible = visible & (qi * tq + tq - 1 - kt * tk < window)

            @pl.when(visible)
            def _():
                softmax_pass(False)

            @pl.when(jnp.logical_not(visible))
            def _():
                build_mask()
                softmax_pass(True)

        vb = kbs[0] if v_is_k0 else v_ref[0].astype(BF16)
        pv = jnp.dot(p_ref[...], vb, preferred_element_type=F32)
        acc_ref[...] = acc_ref[...] * jnp.tile(a_ref[...], (1, dv // LANES)) + pv

    _step()

    @pl.when(tab_ref[3 * nsteps + step] == 1)
    def _fin():
        for h in range(H):
            rows = slice(h * tq, (h + 1) * tq)
            ok = m_ref[rows, :] > 0.5 * NEG
            inv = jnp.where(ok, 1.0 / jnp.where(ok, l_ref[rows, :], 1.0), 0.0)
            o_ref[0, :, h * dv:(h + 1) * dv] = (acc_ref[rows, :] * jnp.tile(inv, (1, dv // LANES))).astype(o_ref.dtype)


def flash(qs, ks, v, *, H, G=1, dks, dv, scale, q_cols=None, k_cols=None, v_col=0, window=0,
          cq=None, ck=None, bm=None, tq=128, tk=512, out_dtype=F32):
    B, S = qs[0].shape[:2]
    n = len(dks)
    q_cols = q_cols or [0] * n
    k_cols = k_cols or [0] * n
    tq = min(tq, S)
    tk = min(tk, S)
    assert tk % LANES == 0 and dv % LANES == 0
    nq = S // tq
    pairs = [(i, t) for i in range(nq) for t in range(_kv_lo(i, tq, tk, window), _kv_hi(i, tq, tk) + 1)]
    nsteps = len(pairs)
    tab = np.array([[i for i, _ in pairs], [t for _, t in pairs],
                    [int(t == _kv_lo(i, tq, tk, window)) for i, t in pairs],
                    [int(t == _kv_hi(i, tq, tk)) for i, t in pairs]], np.int32).reshape(-1)

    def qblk(s, tab):
        return tab[s]

    def kblk(s, tab):
        return tab[nsteps + s]

    in_specs, args = [], []
    for a, dk, c in zip(qs, dks, q_cols):
        in_specs.append(pl.BlockSpec((1, tq, H * dk), lambda b, g, s, tab, c=c: (b, qblk(s, tab), c + g)))
        args.append(a)
    for a, dk, c in zip(ks, dks, k_cols):
        in_specs.append(pl.BlockSpec((1, tk, dk), lambda b, g, s, tab, c=c: (b, kblk(s, tab), c + g)))
        args.append(a)
    if v is not None:
        in_specs.append(pl.BlockSpec((1, tk, dv), lambda b, g, s, tab: (b, kblk(s, tab), v_col + g)))
        args.append(v)
    if cq is not None:
        in_specs.append(pl.BlockSpec((1, 1, tq, H), lambda b, g, s, tab: (b, g, qblk(s, tab), 0)))
        in_specs.append(pl.BlockSpec((1, 1, H, tk), lambda b, g, s, tab: (b, g, 0, kblk(s, tab))))
        args += [cq, ck]
    if bm is not None:
        in_specs.append(pl.BlockSpec((1, tq, bm.shape[-1]), lambda b, g, s, tab: (b, qblk(s, tab), 0)))
        args.append(bm)
    R = H * tq
    scratch = [pltpu.VMEM((R, dk), BF16) for dk in dks]
    scratch += [pltpu.VMEM((R, tk), F32), pltpu.VMEM((R, tk), BF16),
                pltpu.VMEM((R, LANES), F32), pltpu.VMEM((R, LANES), F32), pltpu.VMEM((R, LANES), F32),
                pltpu.VMEM((R, dv), F32), pltpu.VMEM((tq, tk), F32)]
    if cq is not None:
        scratch.append(pltpu.VMEM((R, LANES), F32))
    return pl.pallas_call(
        functools.partial(_flash_body, H=H, tq=tq, tk=tk, dks=tuple(dks), dv=dv, c2=scale * LOG2E,
                          window=window, has_bias=cq is not None, has_bm=bm is not None,
                          v_is_k0=v is None, nsteps=nsteps),
        grid_spec=pltpu.PrefetchScalarGridSpec(
            num_scalar_prefetch=1, grid=(B, G, nsteps), in_specs=in_specs,
            out_specs=pl.BlockSpec((1, tq, H * dv), lambda b, g, s, tab: (b, qblk(s, tab), g)),
            scratch_shapes=scratch),
        out_shape=jax.ShapeDtypeStruct((B, S, G * H * dv), out_dtype),
        compiler_params=_cparams("parallel", "parallel", "arbitrary"),
        name="flash",
    )(jnp.asarray(tab), *args)


def _cmp_topk_body(q_ref, km_ref, vm_ref, oc_ref, bm_ref, p_ref, *, H, tq, ncp, pos0, tvalid, stacked):
    qi = pl.program_id(1)
    if stacked:
        qs = q_ref[0].astype(BF16)
    else:
        qs = jnp.concatenate([q_ref[0, :, h * NSA_DH:(h + 1) * NSA_DH] for h in range(H)],
                             axis=0).astype(BF16)
    s_full = _dot_nt(qs, km_ref[0].astype(BF16))
    r = lax.broadcasted_iota(jnp.int32, (tq, ncp), 0)
    c = lax.broadcasted_iota(jnp.int32, (tq, ncp), 1)
    if tvalid is None:
        pos = pos0 + qi * tq + r
    else:
        pos = pos0 + jnp.minimum(r, tvalid - 1)
    cmask = (c + 1) * NSA_CMP - 1 <= pos
    imp = jnp.zeros((tq, ncp), F32)
    for h in range(H):
        rows = slice(h * tq, (h + 1) * tq)
        s = jnp.where(cmask, s_full[rows] * NSA_SCALE, NEG)
        m = jnp.max(s, axis=-1, keepdims=True)
        p = jnp.where(cmask, jnp.exp(s - m), 0.0)
        p = p / jnp.maximum(jnp.sum(p, axis=-1, keepdims=True), 1e-30)
        imp = imp + p
        p_ref[rows] = p.astype(BF16)
    o = jnp.dot(p_ref[...], vm_ref[0].astype(BF16), preferred_element_type=F32)
    if stacked:
        oc_ref[0] = o.astype(oc_ref.dtype)
    else:
        for h in range(H):
            oc_ref[0, :, h * NSA_DH:(h + 1) * NSA_DH] = o[h * tq:(h + 1) * tq].astype(oc_ref.dtype)
    nxt = pltpu.roll(imp, ncp - 1, axis=1)
    prv = pltpu.roll(imp, 1, axis=1)
    ps = imp + jnp.where((c & 1) == 0, nxt, prv)
    blk = lax.shift_right_logical(c, 1)
    avail = blk * NSA_SEL <= pos
    cur = lax.shift_right_logical(pos, SEL_SHIFT)
    forced = (blk == 0) | (blk == cur) | (blk == cur - 1)
    sc = jnp.where(avail, ps + jnp.where(forced, NSA_FORCE, 0.0), NEG)
    rank = jnp.zeros((tq, ncp), F32)
    for i in range(ncp // 2):
        col = sc[:, 2 * i:2 * i + 1]
        beats = (col > sc) | ((col == sc) & (blk > i))
        rank = rank + jnp.where(beats, 1.0, 0.0)
    bm_ref[0] = jnp.where(avail & (rank < NSA_TOPK), 1.0, 0.0)


def cmp_topk(q, kmean, vmean, *, tq, pos0, tvalid=None, stacked=False):
    B = q.shape[0]
    ncp = kmean.shape[1]
    H = NSA_H
    if stacked:
        nq = 1
        q_spec = pl.BlockSpec((1, H * tq, NSA_DH), lambda b, i: (b, 0, 0))
        o_spec = pl.BlockSpec((1, H * tq, NSA_DH), lambda b, i: (b, 0, 0))
        o_shape = jax.ShapeDtypeStruct((B, H * tq, NSA_DH), F32)
    else:
        nq = q.shape[1] // tq
        q_spec = pl.BlockSpec((1, tq, H * NSA_DH), lambda b, i: (b, i, 0))
        o_spec = pl.BlockSpec((1, tq, H * NSA_DH), lambda b, i: (b, i, 0))
        o_shape = jax.ShapeDtypeStruct((B, nq * tq, H * NSA_DH), F32)
    return pl.pallas_call(
        functools.partial(_cmp_topk_body, H=H, tq=tq, ncp=ncp, pos0=pos0, tvalid=tvalid, stacked=stacked),
        grid=(B, nq),
        in_specs=[q_spec,
                  pl.BlockSpec((1, ncp, NSA_DH), lambda b, i: (b, 0, 0)),
                  pl.BlockSpec((1, ncp, NSA_DH), lambda b, i: (b, 0, 0))],
        out_specs=[o_spec, pl.BlockSpec((1, tq, ncp), lambda b, i: (b, i, 0))],
        out_shape=[o_shape, jax.ShapeDtypeStruct((B, nq * tq, ncp), F32)],
        scratch_shapes=[pltpu.VMEM((H * tq, ncp), BF16)],
        compiler_params=_cparams("parallel", "arbitrary"),
        name="cmp_topk",
    )(q, kmean, vmean)


def _block_mean_body(x_ref, o_ref, *, rows):
    x = x_ref[0].astype(F32)
    o_ref[0] = jnp.sum(x.reshape(rows // NSA_CMP, NSA_CMP, x.shape[-1]), axis=1) * (1.0 / NSA_CMP)


def block_mean(x, *, tr=1024):
    B, S, d = x.shape
    tr = min(tr, S)
    assert S % tr == 0 and (tr // NSA_CMP) % SUBLANES == 0
    return pl.pallas_call(
        functools.partial(_block_mean_body, rows=tr),
        grid=(B, S // tr),
        in_specs=[pl.BlockSpec((1, tr, d), lambda b, i: (b, i, 0))],
        out_specs=pl.BlockSpec((1, tr // NSA_CMP, d), lambda b, i: (b, i, 0)),
        out_shape=jax.ShapeDtypeStruct((B, S // NSA_CMP, d), F32),
        compiler_params=_cparams("parallel", "arbitrary"),
        name="block_mean",
    )(x)


def _pool_mean_body(pt_ref, *refs, pp):
    o_ref = refs[-1]
    per = PAGE // NSA_CMP
    for i in range(pp):
        x = refs[i][...].astype(F32)
        o_ref[0, i * per:(i + 1) * per, :] = jnp.sum(x.reshape(per, NSA_CMP, x.shape[-1]), axis=1) * (1.0 / NSA_CMP)


def pool_block_mean(pool, layer, pt_flat, db, npages, *, pp=32):
    d = pool.shape[-1]
    pp = min(pp, npages)
    assert npages % pp == 0
    per = PAGE // NSA_CMP
    in_specs = [pl.BlockSpec((None, None, PAGE, d),
                             lambda b, c, pt, i=i: (layer, pt[b * npages + c * pp + i], 0, 0))
                for i in range(pp)]
    return pl.pallas_call(
        functools.partial(_pool_mean_body, pp=pp),
        grid_spec=pltpu.PrefetchScalarGridSpec(
            num_scalar_prefetch=1, grid=(db, npages // pp), in_specs=in_specs,
            out_specs=pl.BlockSpec((1, pp * per, d), lambda b, c, pt: (b, c, 0))),
        out_shape=jax.ShapeDtypeStruct((db, npages * per, d), F32),
        compiler_params=_cparams("parallel", "arbitrary"),
        name="pool_block_mean",
    )(pt_flat, *([pool] * pp))


def _cumsum_body(x_ref, c_ref, ct_ref, carry_ref, *, blk):
    x = x_ref[0].astype(F32)
    nh = x.shape[-1]

    @pl.when(pl.program_id(1) == 0)
    def _():
        carry_ref[...] = jnp.zeros(carry_ref.shape, F32)

    r = lax.broadcasted_iota(jnp.int32, (blk, blk), 0)
    cc = lax.broadcasted_iota(jnp.int32, (blk, blk), 1)
    tri = jnp.where(cc <= r, 1.0, 0.0)
    c = jnp.dot(tri, x, preferred_element_type=F32, precision=lax.Precision.HIGHEST) + carry_ref[...]
    carry_ref[...] = c[blk - 1:blk, :]
    c_ref[0] = c
    eye = jnp.where(lax.broadcasted_iota(jnp.int32, (nh, nh), 0) == lax.broadcasted_iota(jnp.int32, (nh, nh), 1),
                    1.0, 0.0)
    ct_ref[0] = lax.dot_general(eye, c, (((1,), (1,)), ((), ())), preferred_element_type=F32,
                                precision=lax.Precision.HIGHEST)


def cumsum_seq(x):
    B, S, nh = x.shape
    blk = min(PAGE, S)
    return pl.pallas_call(
        functools.partial(_cumsum_body, blk=blk),
        grid=(B, S // blk),
        in_specs=[pl.BlockSpec((1, blk, nh), lambda b, i: (b, i, 0))],
        out_specs=[pl.BlockSpec((1, blk, nh), lambda b, i: (b, i, 0)),
                   pl.BlockSpec((1, nh, blk), lambda b, i: (b, 0, i))],
        out_shape=[jax.ShapeDtypeStruct((B, S, nh), F32), jax.ShapeDtypeStruct((B, nh, S), F32)],
        scratch_shapes=[pltpu.VMEM((1, nh), F32)],
        compiler_params=_cparams("parallel", "arbitrary"),
        name="cumsum_seq",
    )(x)


def _cumsum_pages_body(pt_ref, *refs, npages):
    o_ref = refs[-1]
    nh = o_ref.shape[1]
    lane = lax.broadcasted_iota(jnp.int32, (nh, PAGE), 1)
    carry = jnp.zeros((nh, 1), F32)
    for j in range(npages):
        x = refs[j][...].astype(F32)
        sh = 1
        while sh < PAGE:
            x = x + jnp.where(lane >= sh, pltpu.roll(x, sh, axis=1), 0.0)
            sh *= 2
        o_ref[0, :, j * PAGE:(j + 1) * PAGE] = x + carry
        carry = carry + x[:, PAGE - 1:PAGE]


def cumsum_pages(pool_t, layer, pt_flat, db, npages):
    nh = pool_t.shape[2]
    in_specs = [pl.BlockSpec((None, None, nh, PAGE), lambda b, pt, j=j: (layer, pt[b * npages + j], 0, 0))
                for j in range(npages)]
    return pl.pallas_call(
        functools.partial(_cumsum_pages_body, npages=npages),
        grid_spec=pltpu.PrefetchScalarGridSpec(
            num_scalar_prefetch=1, grid=(db,), in_specs=in_specs,
            out_specs=pl.BlockSpec((1, nh, npages * PAGE), lambda b, pt: (b, 0, 0))),
        out_shape=jax.ShapeDtypeStruct((db, nh, npages * PAGE), F32),
        compiler_params=_cparams("parallel"),
        name="cumsum_pages",
    )(pt_flat, *([pool_t] * npages))


def _decode_body(pt_ref, *refs, H, G, dks, ktr, pp, dv, c2, window, has_bias, has_bm, v_is_k0,
                 nchunks, kpos0, qpos0, tvalid):
    nparts = len(dks)
    it = iter(refs)
    q_refs = [next(it) for _ in range(nparts)]
    kp_refs = [[next(it) for _ in range(pp)] for _ in range(nparts)]
    vp_refs = kp_refs[0] if v_is_k0 else [next(it) for _ in range(pp)]
    kn_refs = [next(it) for _ in range(nparts)]
    vn_ref = kn_refs[0] if v_is_k0 else next(it)
    cq_ref = ck_ref = ckn_ref = bm_ref = None
    if has_bias:
        cq_ref, ck_ref, ckn_ref = next(it), next(it), next(it)
    if has_bm:
        bm_ref = next(it)
    o_ref = next(it)
    m_ref, l_ref, acc_ref = next(it), next(it), next(it)
    R = H * TPAD
    c = pl.program_id(1)

    @pl.when(c == 0)
    def _init():
        m_ref[...] = jnp.full(m_ref.shape, NEG, F32)
        l_ref[...] = jnp.zeros(l_ref.shape, F32)
        acc_ref[...] = jnp.zeros(acc_ref.shape, F32)

    def page_rows(ref, g):
        return ref[...] if G == 1 else ref[pl.ds(g, PAGE, stride=G), :]

    GH = G * H
    qbs = [[qr[0, g].astype(BF16) for qr in q_refs] for g in range(G)]

    def rows_cat(parts):
        return parts[0] if G == 1 else jnp.concatenate(parts, axis=0)

    def update(s, kpos, ck, key_ok, causal, pv):
        K = s.shape[-1]
        t3 = (s * c2).reshape(GH, TPAD, K)
        if has_bias:
            t3 = t3 + ((cq_ref[0] * LOG2E).reshape(GH, TPAD, 1) - (ck * LOG2E)[:, None, :])
        mask = key_ok
        if causal or window or has_bm:
            tok = jnp.minimum(lax.broadcasted_iota(jnp.int32, (TPAD, K), 0), tvalid - 1)
            qpos = qpos0 + tok
            if causal:
                cm = kpos <= qpos
                mask = cm if mask is None else mask & cm
            if window:
                wm = (qpos - kpos) < window
                mask = wm if mask is None else mask & wm
            if has_bm:
                bm = bm_ref[0].astype(BF16)
                nc = bm.shape[1]
                cidx = lax.broadcasted_iota(jnp.int32, (nc, K), 0)
                kk = jnp.broadcast_to(kpos[0:1, :], (nc, K))
                expand = jnp.where(cidx == 2 * lax.shift_right_logical(kk, SEL_SHIFT), 1.0, 0.0).astype(BF16)
                sm = jnp.dot(bm, expand, preferred_element_type=F32) > 0.5
                mask = sm if mask is None else mask & sm
        if mask is not None:
            t3 = jnp.where(mask[None], t3, NEG)
        t = t3.reshape(G * R, K)
        m_prev = m_ref[...]
        m_new = jnp.maximum(m_prev, jnp.max(t, axis=-1, keepdims=True))
        alpha = jnp.exp2(m_prev - m_new)
        p = jnp.exp2(t - m_new)
        l_ref[...] = alpha * l_ref[...] + jnp.sum(p, axis=-1, keepdims=True)
        m_ref[...] = m_new
        acc_ref[...] = acc_ref[...] * alpha + pv(p.astype(BF16))

    kb0 = [[] for _ in range(G)]
    s_groups = []
    for g in range(G):
        s_list = []
        for j in range(pp):
            s = None
            for i in range(nparts):
                if ktr[i]:
                    d = jnp.dot(qbs[g][i], kp_refs[i][j][...].astype(BF16), preferred_element_type=F32)
                else:
                    kb = page_rows(kp_refs[i][j], g).astype(BF16)
                    if i == 0:
                        kb0[g].append(kb)
                    d = _dot_nt(qbs[g][i], kb)
                s = d if s is None else s + d
            s_list.append(s)
        s_groups.append(jnp.concatenate(s_list, axis=1) if pp > 1 else s_list[0])
    K = pp * PAGE
    kpos = kpos0 + c * K + lax.broadcasted_iota(jnp.int32, (TPAD, K), 1)

    def pv_past(pb):
        outs = []
        for g in range(G):
            out = None
            for j in range(pp):
                vb = kb0[g][j] if v_is_k0 else page_rows(vp_refs[j], g).astype(BF16)
                d = jnp.dot(pb[g * R:(g + 1) * R, j * PAGE:(j + 1) * PAGE], vb, preferred_element_type=F32)
                out = d if out is None else out + d
            outs.append(out)
        return rows_cat(outs)

    update(rows_cat(s_groups), kpos, ck_ref[0].reshape(GH, K) if has_bias else None, None, False, pv_past)

    @pl.when(c == nchunks - 1)
    def _new_and_fin():
        s_groups, vnbs = [], []
        for g in range(G):
            knb = [kn[0, :, g * dk:(g + 1) * dk].astype(BF16) for kn, dk in zip(kn_refs, dks)]
            s = None
            for i in range(nparts):
                d = _dot_nt(qbs[g][i], knb[i])
                s = d if s is None else s + d
            s_groups.append(s)
            vnbs.append(knb[0] if v_is_k0 else vn_ref[0, :, g * dv:(g + 1) * dv].astype(BF16))
        j = lax.broadcasted_iota(jnp.int32, (TPAD, TPAD), 1)

        def pv_new(pb):
            return rows_cat([jnp.dot(pb[g * R:(g + 1) * R], vnbs[g], preferred_element_type=F32)
                             for g in range(G)])

        update(rows_cat(s_groups), qpos0 + j, ckn_ref[0].reshape(GH, TPAD) if has_bias else None,
               j < tvalid, True, pv_new)
        ok = m_ref[...] > 0.5 * NEG
        o = jnp.where(ok, acc_ref[...] / jnp.where(ok, l_ref[...], 1.0), 0.0)
        o_ref[0] = o.reshape(G, R, dv).astype(o_ref.dtype)


def decode(qs, pools, vpool, layer, pt_flat, npages, knews, vnew, *, H, G=1, dks, dv, scale,
           qpos0, kpos0=0, tvalid, ktr=None, window=0, cq=None, ck=None, cknew=None, bm=None, pp=32):
    DB = qs[0].shape[0]
    nparts = len(dks)
    ktr = ktr or (False,) * nparts
    pp = min(pp, npages)
    assert npages % pp == 0
    nchunks = npages // pp
    R = H * TPAD
    in_specs, args = [], []
    for a, dk in zip(qs, dks):
        in_specs.append(pl.BlockSpec((1, G, R, dk), lambda b, c, pt: (b, 0, 0, 0)))
        args.append(a)

    def page_specs(shape):
        return [pl.BlockSpec((None, None) + shape,
                             lambda b, c, pt, j=j: (layer, pt[b * npages + c * pp + j], 0, 0))
                for j in range(pp)]

    for a, dk, tr in zip(pools, dks, ktr):
        in_specs += page_specs((dk, PAGE) if tr else (PAGE * G, dk))
        args += [a] * pp
    if vpool is not None:
        in_specs += page_specs((PAGE * G, dv))
        args += [vpool] * pp
    for a, dk in zip(knews, dks):
        in_specs.append(pl.BlockSpec((1, TPAD, G * dk), lambda b, c, pt: (b, 0, 0)))
        args.append(a)
    if vnew is not None:
        in_specs.append(pl.BlockSpec((1, TPAD, G * dv), lambda b, c, pt: (b, 0, 0)))
        args.append(vnew)
    if cq is not None:
        in_specs.append(pl.BlockSpec((1, G, R, 1), lambda b, c, pt: (b, 0, 0, 0)))
        in_specs.append(pl.BlockSpec((1, G, H, pp * PAGE), lambda b, c, pt: (b, 0, 0, c)))
        in_specs.append(pl.BlockSpec((1, G, H, TPAD), lambda b, c, pt: (b, 0, 0, 0)))
        args += [cq, ck, cknew]
    if bm is not None:
        in_specs.append(pl.BlockSpec((1, TPAD, bm.shape[-1]), lambda b, c, pt: (b, 0, 0)))
        args.append(bm)
    return pl.pallas_call(
        functools.partial(_decode_body, H=H, G=G, dks=tuple(dks), ktr=tuple(ktr), pp=pp, dv=dv,
                          c2=scale * LOG2E, window=window, has_bias=cq is not None, has_bm=bm is not None,
                          v_is_k0=vpool is None, nchunks=nchunks, kpos0=kpos0, qpos0=qpos0, tvalid=tvalid),
        grid_spec=pltpu.PrefetchScalarGridSpec(
            num_scalar_prefetch=1, grid=(DB, nchunks), in_specs=in_specs,
            out_specs=pl.BlockSpec((1, G, R, dv), lambda b, c, pt: (b, 0, 0, 0)),
            scratch_shapes=[pltpu.VMEM((G * R, 1), F32), pltpu.VMEM((G * R, 1), F32),
                            pltpu.VMEM((G * R, dv), F32)]),
        out_shape=jax.ShapeDtypeStruct((DB, G, R, dv), F32),
        compiler_params=_cparams("parallel", "arbitrary"),
        name="decode",
    )(pt_flat, *args)


def _nsa_mix_body(oc_ref, os_ref, ow_ref, gt_ref, g_ref, o_ref):
    gt = gt_ref[...]
    for h in range(NSA_H):
        cols = slice(h * NSA_DH, (h + 1) * NSA_DH)
        o = (gt[:, h:h + 1] * oc_ref[:, cols] + gt[:, NSA_H + h:NSA_H + h + 1] * os_ref[:, cols]
             + gt[:, 2 * NSA_H + h:2 * NSA_H + h + 1] * ow_ref[:, cols])
        g = g_ref[:, cols]
        o_ref[:, cols] = (o * (g * jax.nn.sigmoid(g))).astype(o_ref.dtype)


def nsa_mix(oc, osel, ow, gates, proj, g_col, *, tm=256):
    M, W = oc.shape
    tm = min(tm, M)
    assert M % tm == 0
    spec = pl.BlockSpec((tm, W), lambda i: (i, 0))
    return pl.pallas_call(
        _nsa_mix_body,
        grid=(M // tm,),
        in_specs=[spec, spec, spec, pl.BlockSpec((tm, gates.shape[1]), lambda i: (i, 0)),
                  pl.BlockSpec((tm, W), lambda i: (i, g_col))],
        out_specs=spec,
        out_shape=jax.ShapeDtypeStruct((M, W), BF16),
        compiler_params=_cparams("parallel"),
        name="nsa_mix",
    )(oc, osel, ow, gates, proj)


def _shift_append_body(buf_ref, new_ref, o_ref, *, wb, t):
    o_ref[:, 0:wb - t, :] = buf_ref[:, t:wb, :]
    o_ref[:, wb - t:wb, :] = new_ref[:, 0:t, :]


def shift_append(state, layer, new, t, *, tb=8):
    _, DB, wb, d = state.shape
    tb = min(tb, DB)
    assert DB % tb == 0
    return pl.pallas_call(
        functools.partial(_shift_append_body, wb=wb, t=t),
        grid=(DB // tb,),
        in_specs=[pl.BlockSpec((None, tb, wb, d), lambda i: (layer, i, 0, 0)),
                  pl.BlockSpec((tb, TPAD, d), lambda i: (i, 0, 0))],
        out_specs=pl.BlockSpec((tb, wb, d), lambda i: (i, 0, 0)),
        out_shape=jax.ShapeDtypeStruct((DB, wb, d), state.dtype),
        compiler_params=_cparams("parallel"),
        name="shift_append",
    )(state, new)


def _rope(x, pos):
    half = x.shape[-1] // 2
    inv = ROPE_THETA ** (-jnp.arange(half, dtype=F32) / half)
    ang = pos.astype(F32)[:, None] * inv[None, :]
    cos = jnp.cos(ang)[:, None, :]
    sin = jnp.sin(ang)[:, None, :]
    x1, x2 = x[..., :half], x[..., half:]
    return jnp.concatenate([x1 * cos - x2 * sin, x1 * sin + x2 * cos], axis=-1)


def _pad_rows(x, n, axis):
    pad = [(0, 0)] * x.ndim
    pad[axis] = (0, n - x.shape[axis])
    return jnp.pad(x, pad)


def _stack_rows(x, H, d):
    DB, T, _ = x.shape
    x = x.reshape(DB, T, H, d).transpose(0, 2, 1, 3)
    return _pad_rows(x, TPAD, 2).reshape(DB, 1, H * TPAD, d)


def _unstack_rows(o, H, T):
    DB, _, d = o.shape
    return o.reshape(DB, H, TPAD, d)[:, :, :T].transpose(0, 2, 1, 3).reshape(DB, T, H * d)


def _even_weights(w_in, w_uq, w_uk, w_uv, w_out):
    D = w_in.shape[0]
    o = np.cumsum((0, MLA_QL, MLA_KVL, MLA_ROPE, MLA_H * MLA_V, FOX_H * FOX_DH, FOX_KVH * FOX_DH,
                   FOX_KVH * FOX_DH, FOX_H, FOX_H * FOX_DH))
    seg = lambda i: w_in[:, o[i]:o[i + 1]]
    order = [seg(4), seg(3), seg(8), seg(0), seg(1), seg(5), seg(6), seg(2), seg(7)]
    n = sum(s.shape[1] for s in order)
    w = jnp.concatenate(order + [jnp.zeros((D, _round_up(n, 512) - n), w_in.dtype)], axis=1).astype(BF16)
    uq = jnp.concatenate([w_uq[:, :, :MLA_NOPE].reshape(MLA_QL, -1), w_uq[:, :, MLA_NOPE:].reshape(MLA_QL, -1)],
                         axis=1).astype(BF16)
    uk = w_uk.transpose(1, 2, 0).astype(BF16)
    uv = w_uv.transpose(1, 0, 2).astype(BF16)
    return w, uq, uk, uv, w_out.astype(BF16)


EV_COL = dict(fq=0, gm=1024, gf=2048, cq=3072, ckv=3584, fk=3840, fv=4096, kr=4352, fl=4416)


def _odd_weights(w_in, gate_bias, w_out):
    D = w_in.shape[0]
    W = NSA_H * NSA_DH
    q, kv, gl, g = w_in[:, :W], w_in[:, W:W + 6 * NSA_DH], w_in[:, W + 6 * NSA_DH:W + 6 * NSA_DH + 3 * NSA_H], \
        w_in[:, W + 6 * NSA_DH + 3 * NSA_H:]
    gl = gl.reshape(D, NSA_H, 3).transpose(0, 2, 1).reshape(D, 3 * NSA_H)
    n = 2 * W + 6 * NSA_DH + 3 * NSA_H
    w = jnp.concatenate([q, g, kv, gl, jnp.zeros((D, _round_up(n, 512) - n), w_in.dtype)], axis=1).astype(BF16)
    gb = gate_bias.reshape(NSA_H, 3).T.reshape(3 * NSA_H)
    return w, gb, w_out.astype(BF16)


def kernel(x_prompt, x_sample, cache_mla_ckv, cache_mla_krope, cache_fox_k, cache_fox_v, cache_fox_logf, cache_nsa_cmp_k, cache_nsa_cmp_v, cache_nsa_sel_k, cache_nsa_sel_v, state_nsa_win_k, state_nsa_win_v, page_table, c_prompt, c_sample, ada_w, ada_b, norm_g, final_g, ev_w_in, mla_q_norm, mla_kv_norm, mla_w_uq, mla_w_uk, mla_w_uv, fox_f_bias, ev_w_out, od_w_in, nsa_gate_bias, od_w_out):
    B, S, D = x_prompt.shape
    DB, T, _ = x_sample.shape
    depth = ada_w.shape[0]
    npages = page_table.shape[1]
    P = npages * PAGE
    MS = DB * T
    pos_p = jnp.arange(S)
    pos_s = P + jnp.arange(T)
    pt_flat = page_table.reshape(-1).astype(jnp.int32)

    c_all = jnp.concatenate([c_prompt, c_sample], axis=0)
    mc = _round_up(B + DB, SUBLANES)
    c_all = _pad_rows(c_all, mc, 0)[None]
    fox_pool_k = cache_fox_k.reshape(cache_fox_k.shape[:2] + (PAGE * FOX_KVH, FOX_DH))
    fox_pool_v = cache_fox_v.reshape(cache_fox_v.shape[:2] + (PAGE * FOX_KVH, FOX_DH))
    krope_pool_t = jnp.swapaxes(cache_mla_krope, 2, 3)
    logf_pool_t = jnp.swapaxes(cache_fox_logf, 2, 3)

    xp = x_prompt
    xs = x_sample.reshape(1, MS, D)
    outs = {k: [] for k in ('ckv_p', 'ckv_s', 'kr_p', 'kr_s', 'fk_p', 'fk_s', 'fv_p', 'fv_s', 'lf_p', 'lf_s',
                            'kc_p', 'kc_s', 'vc_p', 'vc_s', 'ks_p', 'ks_s', 'vs_p', 'vs_s',
                            'wk_p', 'wk_s', 'wv_p', 'wv_s')}

    for l in range(depth):
        mod = mm(c_all, ada_w, w_idx=l, bias=ada_b[l][None])[0]
        shift, scale, gate = mod[:, :D], mod[:, D:2 * D], mod[:, 2 * D:]
        rep = lambda a: jnp.repeat(a[B:B + DB], T, axis=0)[None]
        hp = norm(xp, norm_g[l], scale[:B, None], shift[:B, None], out_dtype=BF16)
        hs = norm(xs, norm_g[l], rep(scale), rep(shift), out_dtype=BF16)
        gate_p, gate_s = gate[:B, None], rep(gate)

        if l % 2 == 0:
            e = l // 2
            w_in, w_uq, w_uk, w_uv, w_out = _even_weights(ev_w_in[e], mla_w_uq[e], mla_w_uk[e], mla_w_uv[e],
                                                          ev_w_out[e])
            C = EV_COL

            def project(h, pos):
                proj = mm(h, w_in)
                g_, r_, _ = proj.shape
                cqn = norm(proj, mla_q_norm[e], col=C['cq'] // MLA_QL, width=MLA_QL, out_dtype=BF16)
                q = mm(cqn, w_uq)
                q_lat = mm_heads(q.reshape(g_ * r_, -1), w_uk).reshape(g_, r_, -1)
                q_rope = _rope(q[..., MLA_H * MLA_NOPE:].reshape(g_, r_, MLA_H, MLA_ROPE), pos).reshape(g_, r_, -1)
                ckv = norm(proj, mla_kv_norm[e], col=C['ckv'] // MLA_KVL, width=MLA_KVL)
                kr = _rope(proj[..., None, C['kr']:C['kr'] + MLA_ROPE], pos)[..., 0, :]
                logf = jax.nn.log_sigmoid(proj[..., C['fl']:C['fl'] + FOX_H] + fox_f_bias[e])
                return proj, q_lat, q_rope, ckv, kr, logf

            def output(proj, o_lat, o_fox, x, gate_):
                g_, r_, _ = proj.shape
                o_mla = mm_heads(o_lat.reshape(g_ * r_, -1), w_uv).reshape(g_, r_, -1)
                gm = proj[..., C['gm']:C['gm'] + MLA_H * MLA_V]
                gf = proj[..., C['gf']:C['gf'] + FOX_H * FOX_DH]
                mix = jnp.concatenate([o_mla * jax.nn.silu(gm), o_fox * jax.nn.silu(gf)], axis=-1).astype(BF16)
                return mm(mix, w_out, res=x, gate=gate_)

            proj, q_lat, q_rope, ckv, kr, logf = project(hp, pos_p)
            o_lat = flash([q_lat, q_rope], [ckv, kr], None, H=MLA_H, dks=(MLA_KVL, MLA_ROPE), dv=MLA_KVL,
                          scale=MLA_SCALE, tq=256)
            csum, csum_t = cumsum_seq(logf)
            cq = csum.reshape(B, S, FOX_KVH, FOX_G).transpose(0, 2, 1, 3)
            ck = csum_t.reshape(B, FOX_KVH, FOX_G, S)
            kb = C['fk'] // FOX_DH
            vb = C['fv'] // FOX_DH
            o_fox = flash([proj], [proj], proj, H=FOX_G, G=FOX_KVH, dks=(FOX_DH,), dv=FOX_DH, scale=FOX_SCALE,
                          q_cols=[0], k_cols=[kb], v_col=vb, cq=cq, ck=ck, tq=256)
            fk = proj[..., C['fk']:C['fk'] + FOX_KVH * FOX_DH]
            fv = proj[..., C['fv']:C['fv'] + FOX_KVH * FOX_DH]
            outs['ckv_p'].append(ckv)
            outs['kr_p'].append(kr)
            outs['fk_p'].append(fk.reshape(B, S, FOX_KVH, FOX_DH))
            outs['fv_p'].append(fv.reshape(B, S, FOX_KVH, FOX_DH))
            outs['lf_p'].append(logf)
            xp = output(proj, o_lat, o_fox, xp, gate_p)

            proj, q_lat, q_rope, ckv, kr, logf = project(hs, jnp.tile(pos_s, DB))
            ckv3, kr3 = ckv.reshape(DB, T, -1), kr.reshape(DB, T, -1)
            o_lat = decode([_stack_rows(q_lat.reshape(DB, T, -1), MLA_H, MLA_KVL),
                            _stack_rows(q_rope.reshape(DB, T, -1), MLA_H, MLA_ROPE)],
                           [cache_mla_ckv, krope_pool_t], None, e, pt_flat, npages,
                           [_pad_rows(ckv3, TPAD, 1), _pad_rows(kr3, TPAD, 1)], None,
                           H=MLA_H, dks=(MLA_KVL, MLA_ROPE), ktr=(False, True), dv=MLA_KVL, scale=MLA_SCALE,
                           qpos0=P, tvalid=T)
            o_lat = _unstack_rows(o_lat[:, 0], MLA_H, T).reshape(1, MS, -1)
            cpast_t = cumsum_pages(logf_pool_t, e, pt_flat, DB, npages)
            lf3 = logf.reshape(DB, T, FOX_H)
            c_new = cpast_t[:, :, -1][:, None, :] + jnp.cumsum(lf3, axis=1)
            fq = proj[0, :, :FOX_H * FOX_DH].reshape(DB, T, FOX_KVH, FOX_G, FOX_DH).transpose(0, 2, 3, 1, 4)
            fq = _pad_rows(fq, TPAD, 3).reshape(DB, FOX_KVH, FOX_G * TPAD, FOX_DH)
            cqs = _pad_rows(c_new.reshape(DB, T, FOX_KVH, FOX_G).transpose(0, 2, 3, 1), TPAD, 3)
            fk = proj[0, :, C['fk']:C['fk'] + FOX_KVH * FOX_DH].reshape(DB, T, -1)
            fv = proj[0, :, C['fv']:C['fv'] + FOX_KVH * FOX_DH].reshape(DB, T, -1)
            o_fox = decode([fq], [fox_pool_k], fox_pool_v, e, pt_flat, npages,
                           [_pad_rows(fk, TPAD, 1)], _pad_rows(fv, TPAD, 1),
                           H=FOX_G, G=FOX_KVH, dks=(FOX_DH,), dv=FOX_DH, scale=FOX_SCALE, qpos0=P, tvalid=T,
                           cq=cqs.reshape(DB, FOX_KVH, FOX_G * TPAD, 1),
                           ck=cpast_t.reshape(DB, FOX_KVH, FOX_G, P), cknew=cqs)
            o_fox = o_fox.reshape(DB, FOX_KVH, FOX_G, TPAD, FOX_DH)[:, :, :, :T].transpose(0, 3, 1, 2, 4)
            o_fox = o_fox.reshape(1, MS, FOX_H * FOX_DH)
            outs['ckv_s'].append(ckv3)
            outs['kr_s'].append(kr3)
            outs['fk_s'].append(fk.reshape(DB, T, FOX_KVH, FOX_DH))
            outs['fv_s'].append(fv.reshape(DB, T, FOX_KVH, FOX_DH))
            outs['lf_s'].append(lf3)
            xs = output(proj, o_lat, o_fox, xs, gate_s)
        else:
            od = l // 2
            w_in, gbias, w_out = _odd_weights(od_w_in[od], nsa_gate_bias[od], od_w_out[od])
            W = NSA_H * NSA_DH
            KV0 = 2 * W

            def project(h, pos):
                proj = mm(h, w_in)
                g_, r_, _ = proj.shape
                q = _rope(proj[..., :W].reshape(g_, r_, NSA_H, NSA_DH), pos).reshape(g_, r_, W)
                kv = [proj[..., KV0 + i * NSA_DH:KV0 + (i + 1) * NSA_DH] for i in range(6)]
                kc, vc, ks, vs, kw, vw = kv
                kc, ks, kw = (_rope(k[..., None, :], pos)[..., 0, :] for k in (kc, ks, kw))
                gates = jax.nn.sigmoid(proj[..., KV0 + 6 * NSA_DH:KV0 + 6 * NSA_DH + 3 * NSA_H] + gbias)
                return proj, q, kc, vc, ks, vs, kw, vw, gates

            def output(proj, o_c, o_s, o_w, gates, x, gate_):
                g_, r_, _ = proj.shape
                mix = nsa_mix(o_c.reshape(g_ * r_, W), o_s.reshape(g_ * r_, W), o_w.reshape(g_ * r_, W),
                              gates.reshape(g_ * r_, -1), proj.reshape(g_ * r_, -1), 1)
                return mm(mix.reshape(g_, r_, W), w_out, res=x, gate=gate_)

            proj, q, kc, vc, ks, vs, kw, vw, gates = project(hp, pos_p)
            ncp = _round_up(S // NSA_CMP, LANES)
            kmean = _pad_rows(block_mean(kc, tr=min(1024, S)), ncp, 1)
            vmean = _pad_rows(block_mean(vc, tr=min(1024, S)), ncp, 1)
            tq = min(128, S)
            o_c, bm = cmp_topk(q, kmean, vmean, tq=tq, pos0=0)
            o_s = flash([q], [ks], vs, H=NSA_H, dks=(NSA_DH,), dv=NSA_DH, scale=NSA_SCALE, bm=bm, tq=tq)
            o_w = flash([q], [kw], vw, H=NSA_H, dks=(NSA_DH,), dv=NSA_DH, scale=NSA_SCALE, window=NSA_WIN,
                        tq=tq, tk=tq)
            wbp = min(NSA_WIN, S)
            for n_, v_ in zip(('kc_p', 'vc_p', 'ks_p', 'vs_p', 'wk_p', 'wv_p'),
                              (kc, vc, ks, vs, kw[:, S - wbp:], vw[:, S - wbp:])):
                outs[n_].append(v_)
            xp = output(proj, o_c, o_s, o_w, gates, xp, gate_p)

            proj, q, kc, vc, ks, vs, kw, vw, gates = project(hs, jnp.tile(pos_s, DB))
            r3 = lambda a: a.reshape(DB, T, -1)
            kc, vc, ks, vs, kw, vw = map(r3, (kc, vc, ks, vs, kw, vw))
            qst = _stack_rows(r3(q), NSA_H, NSA_DH)
            ncp = _round_up((P + T + NSA_SEL - 1) // NSA_SEL * 2, LANES)
            kmean = _pad_rows(pool_block_mean(cache_nsa_cmp_k, od, pt_flat, DB, npages), ncp, 1)
            vmean = _pad_rows(pool_block_mean(cache_nsa_cmp_v, od, pt_flat, DB, npages), ncp, 1)
            o_c, bm = cmp_topk(qst[:, 0], kmean, vmean, tq=TPAD, pos0=P, tvalid=T, stacked=True)
            pad8 = lambda a: _pad_rows(a, TPAD, 1)
            o_s = decode([qst], [cache_nsa_sel_k], cache_nsa_sel_v, od, pt_flat, npages, [pad8(ks)], pad8(vs),
                         H=NSA_H, dks=(NSA_DH,), dv=NSA_DH, scale=NSA_SCALE, qpos0=P, tvalid=T, bm=bm)
            wb = state_nsa_win_k.shape[2]
            wpages = wb // PAGE
            wtab = jnp.arange(DB * wpages, dtype=jnp.int32)
            wshape = (state_nsa_win_k.shape[0], DB * wpages, PAGE, NSA_DH)
            o_w = decode([qst], [state_nsa_win_k.reshape(wshape)], state_nsa_win_v.reshape(wshape), od, wtab,
                         wpages, [pad8(kw)], pad8(vw), H=NSA_H, dks=(NSA_DH,), dv=NSA_DH, scale=NSA_SCALE,
                         qpos0=P, kpos0=P - wb, tvalid=T, window=NSA_WIN, pp=wpages)
            un = lambda o: _unstack_rows(o, NSA_H, T).reshape(1, MS, W)
            bk = shift_append(state_nsa_win_k, od, pad8(kw), T)
            bv = shift_append(state_nsa_win_v, od, pad8(vw), T)
            for n_, v_ in zip(('kc_s', 'vc_s', 'ks_s', 'vs_s', 'wk_s', 'wv_s'), (kc, vc, ks, vs, bk, bv)):
                outs[n_].append(v_)
            xs = output(proj, un(o_c), un(o_s[:, 0]), un(o_w[:, 0]), gates, xs, gate_s)

    y_prompt = norm(xp, final_g)
    y_sample = norm(xs, final_g).reshape(DB, T, D)
    st = {k: jnp.stack(v) for k, v in outs.items()}
    return (y_prompt, y_sample, st['ckv_p'], st['ckv_s'], st['kr_p'], st['kr_s'],
            st['fk_p'], st['fk_s'], st['fv_p'], st['fv_s'], st['lf_p'], st['lf_s'],
            st['kc_p'], st['kc_s'], st['vc_p'], st['vc_s'], st['ks_p'], st['ks_s'],
            st['vs_p'], st['vs_s'], st['wk_p'], st['wk_s'], st['wv_p'], st['wv_s'])
```

```python
import functools
import math

import numpy as np
import jax
import jax.numpy as jnp
from jax import lax
from jax.experimental import pallas as pl
from jax.experimental.pallas import tpu as pltpu

F32 = jnp.float32
BF16 = jnp.bfloat16
NEG = -1e30
LOG2E = math.log2(math.e)
VMEM_LIMIT = 48 * 1024 * 1024
LANES = 128
SUBLANES = 8

EPS = 1e-6
ROPE_THETA = 10000.0
PAGE = 128
MLA_H, MLA_NOPE, MLA_ROPE, MLA_V, MLA_QL, MLA_KVL = 8, 128, 64, 128, 512, 256
MLA_SCALE = (MLA_NOPE + MLA_ROPE) ** -0.5
FOX_H, FOX_KVH, FOX_DH = 8, 2, 128
FOX_G = FOX_H // FOX_KVH
FOX_SCALE = FOX_DH ** -0.5
NSA_H, NSA_DH, NSA_CMP, NSA_SEL, NSA_TOPK, NSA_WIN = 16, 128, 32, 64, 16, 512
NSA_FORCE = 1000.0
NSA_SCALE = NSA_DH ** -0.5
SEL_SHIFT = 6
LPP = PAGE // NSA_CMP
LPB = NSA_SEL // NSA_CMP
TPAD = SUBLANES
ROW_CHUNK = 16


def _cparams(*sem):
    return pltpu.CompilerParams(dimension_semantics=sem, vmem_limit_bytes=VMEM_LIMIT)


def _dot_nt(a, b):
    return lax.dot_general(a, b, (((1,), (1,)), ((), ())), preferred_element_type=F32)


def _round_up(n, m):
    return -(-n // m) * m


def _mm_body(x_ref, w_ref, *rest, has_bias, has_res):
    o_ref = rest[-1]
    acc = jnp.dot(x_ref[0].astype(BF16), w_ref[...].astype(BF16), preferred_element_type=F32)
    k = 0
    if has_bias:
        acc = acc + rest[k][...]
        k += 1
    if has_res:
        acc = rest[k][0] + rest[k + 1][0] * acc
    o_ref[0] = acc.astype(o_ref.dtype)


def mm(x, w, *, w_idx=None, bias=None, res=None, gate=None, out_dtype=F32, tm=1024, tn=512):
    G, R, K = x.shape
    N = w.shape[-1]
    tm = min(tm, R)
    tn = min(tn, N)
    assert R % tm == 0 and N % tn == 0, (R, tm, N, tn)
    in_specs = [pl.BlockSpec((1, tm, K), lambda g, i, j: (g, i, 0))]
    if w.ndim == 3:
        in_specs.append(pl.BlockSpec((None, K, tn), lambda g, i, j: (w_idx, 0, j)))
    else:
        in_specs.append(pl.BlockSpec((K, tn), lambda g, i, j: (0, j)))
    args = [x, w]
    if bias is not None:
        in_specs.append(pl.BlockSpec((1, tn), lambda g, i, j: (0, j)))
        args.append(bias)
    if res is not None:
        in_specs.append(pl.BlockSpec((1, tm, tn), lambda g, i, j: (g, i, j)))
        args.append(res)
        if gate.shape[1] == 1:
            in_specs.append(pl.BlockSpec((1, 1, tn), lambda g, i, j: (g, 0, j)))
        else:
            in_specs.append(pl.BlockSpec((1, tm, tn), lambda g, i, j: (g, i, j)))
        args.append(gate)
    return pl.pallas_call(
        functools.partial(_mm_body, has_bias=bias is not None, has_res=res is not None),
        grid=(G, R // tm, N // tn),
        in_specs=in_specs,
        out_specs=pl.BlockSpec((1, tm, tn), lambda g, i, j: (g, i, j)),
        out_shape=jax.ShapeDtypeStruct((G, R, N), out_dtype),
        compiler_params=_cparams("parallel", "parallel", "arbitrary"),
        name="mm",
    )(*args)


def _mm_heads_body(x_ref, w_ref, o_ref):
    o_ref[...] = jnp.dot(x_ref[...].astype(BF16), w_ref[0].astype(BF16),
                         preferred_element_type=F32).astype(o_ref.dtype)


def mm_heads(x, w, *, col_off=0, out_dtype=F32, tm=2048):
    M = x.shape[0]
    H, dk, dn = w.shape
    tm = min(tm, M)
    assert M % tm == 0
    return pl.pallas_call(
        _mm_heads_body,
        grid=(M // tm, H),
        in_specs=[pl.BlockSpec((tm, dk), lambda i, h: (i, col_off + h)),
                  pl.BlockSpec((1, dk, dn), lambda i, h: (h, 0, 0))],
        out_specs=pl.BlockSpec((tm, dn), lambda i, h: (i, h)),
        out_shape=jax.ShapeDtypeStruct((M, H * dn), out_dtype),
        compiler_params=_cparams("parallel", "arbitrary"),
        name="mm_heads",
    )(x, w)


def _norm_body(x_ref, g_ref, *rest, mod):
    o_ref = rest[-1]
    xf = x_ref[0].astype(F32)
    y = xf * lax.rsqrt(jnp.mean(xf * xf, axis=-1, keepdims=True) + EPS) * g_ref[...]
    if mod:
        y = y * (1.0 + rest[0][0]) + rest[1][0]
    o_ref[0] = y.astype(o_ref.dtype)


def norm(x, g, scale=None, shift=None, *, col=0, width=None, out_dtype=F32, tr=256):
    G, R, _ = x.shape
    width = width or x.shape[-1]
    tr = min(tr, R)
    assert R % tr == 0
    in_specs = [pl.BlockSpec((1, tr, width), lambda a, i: (a, i, col)),
                pl.BlockSpec((1, width), lambda a, i: (0, 0))]
    args = [x, g.reshape(1, width)]
    mod = scale is not None
    if mod:
        for arr in (scale, shift):
            if arr.shape[1] == 1:
                in_specs.append(pl.BlockSpec((1, 1, width), lambda a, i: (a, 0, 0)))
            else:
                in_specs.append(pl.BlockSpec((1, tr, width), lambda a, i: (a, i, 0)))
            args.append(arr)
    return pl.pallas_call(
        functools.partial(_norm_body, mod=mod),
        grid=(G, R // tr),
        in_specs=in_specs,
        out_specs=pl.BlockSpec((1, tr, width), lambda a, i: (a, i, 0)),
        out_shape=jax.ShapeDtypeStruct((G, R, width), out_dtype),
        compiler_params=_cparams("parallel", "arbitrary"),
        name="norm",
    )(*args)


def _kv_hi(qi, tq, tk):
    return (qi * tq + tq - 1) // tk


def _kv_lo(qi, tq, tk, window):
    return max(qi * tq - (window - 1), 0) // tk if window else 0


def _flash_body(tab_ref, *refs, H, tq, tk, dks, dv, c2, window, has_bias, has_bm, v_is_k0, nsteps):
    it = iter(refs)
    q_refs = [next(it) for _ in dks]
    k_refs = [next(it) for _ in dks]
    v_ref = None if v_is_k0 else next(it)
    cq_ref = ck_ref = bm_ref = None
    if has_bias:
        cq_ref, ck_ref = next(it), next(it)
    if has_bm:
        bm_ref = next(it)
    o_ref = next(it)
    qs_refs = [next(it) for _ in dks]
    s_ref, p_ref, m_ref, l_ref, a_ref, acc_ref, madd_ref = (next(it) for _ in range(7))
    cqb_ref = next(it) if has_bias else None
    rc = min(ROW_CHUNK, tq)

    step = pl.program_id(2)
    qi = tab_ref[step]
    kt = tab_ref[nsteps + step]

    @pl.when(tab_ref[2 * nsteps + step] == 1)
    def _init():
        for qr, qs, dk in zip(q_refs, qs_refs, dks):
            for h in range(H):
                qs[h * tq:(h + 1) * tq, :] = qr[0, :, h * dk:(h + 1) * dk].astype(BF16)
        if has_bias:
            for h in range(H):
                cqb_ref[h * tq:(h + 1) * tq, :] = jnp.broadcast_to(cq_ref[0, 0, :, h:h + 1] * LOG2E, (tq, LANES))
        m_ref[...] = jnp.full(m_ref.shape, NEG, F32)
        l_ref[...] = jnp.zeros(l_ref.shape, F32)
        acc_ref[...] = jnp.zeros(acc_ref.shape, F32)

    def build_mask():
        qpos = qi * tq + lax.broadcasted_iota(jnp.int32, (tq, tk), 0)
        kpos = kt * tk + lax.broadcasted_iota(jnp.int32, (tq, tk), 1)
        mask = kpos <= qpos
        if window:
            mask = mask & ((qpos - kpos) < window)
        if has_bm:
            bm = bm_ref[0].astype(BF16)
            nc = bm.shape[1]
            cidx = lax.broadcasted_iota(jnp.int32, (nc, tk), 0)
            kk = kt * tk + lax.broadcasted_iota(jnp.int32, (nc, tk), 1)
            expand = jnp.where(cidx == 2 * lax.shift_right_logical(kk, SEL_SHIFT), 1.0, 0.0).astype(BF16)
            mask = mask & (jnp.dot(bm, expand, preferred_element_type=F32) > 0.5)
        madd_ref[...] = jnp.where(mask, 0.0, NEG)

    def softmax_pass(masked):
        for h in range(H):
            if has_bias:
                ck_row = ck_ref[0, 0, h:h + 1, :] * LOG2E
            for r in range(tq // rc):
                pr = slice(r * rc, (r + 1) * rc)
                rows = slice(h * tq + r * rc, h * tq + (r + 1) * rc)
                t = s_ref[rows, :] * c2
                if has_bias:
                    t = (t + jnp.tile(cqb_ref[rows, :], (1, tk // LANES))) - ck_row
                if masked:
                    t = t + madd_ref[pr, :]
                s_ref[rows, :] = t
                m_prev = m_ref[rows, :]
                m_new = jnp.maximum(m_prev, jnp.max(t, axis=1, keepdims=True))
                a_ref[rows, :] = jnp.exp2(m_prev - m_new)
                m_ref[rows, :] = m_new
        for h in range(H):
            for r in range(tq // rc):
                rows = slice(h * tq + r * rc, h * tq + (r + 1) * rc)
                p = jnp.exp2(s_ref[rows, :] - jnp.tile(m_ref[rows, :], (1, tk // LANES)))
                l_ref[rows, :] = a_ref[rows, :] * l_ref[rows, :] + jnp.sum(p, axis=1, keepdims=True)
                p_ref[rows, :] = p.astype(BF16)

    def _step():
        kbs = [kr[0].astype(BF16) for kr in k_refs]
        s_full = None
        for qs, kb in zip(qs_refs, kbs):
            d = _dot_nt(qs[...], kb)
            s_full = d if s_full is None else s_full + d
        s_ref[...] = s_full
        if has_bm:
            build_mask()
            softmax_pass(True)
        else:
            visible = kt * tk + tk - 1 <= qi * tq
            if window:
                visible = visible & (qi * tq + tq - 1 - kt * tk < window)

            @pl.when(visible)
            def _():
                softmax_pass(False)

            @pl.when(jnp.logical_not(visible))
            def _():
                build_mask()
                softmax_pass(True)

        vb = kbs[0] if v_is_k0 else v_ref[0].astype(BF16)
        pv = jnp.dot(p_ref[...], vb, preferred_element_type=F32)
        acc_ref[...] = acc_ref[...] * jnp.tile(a_ref[...], (1, dv // LANES)) + pv

    _step()

    @pl.when(tab_ref[3 * nsteps + step] == 1)
    def _fin():
        for h in range(H):
            rows = slice(h * tq, (h + 1) * tq)
            ok = m_ref[rows, :] > 0.5 * NEG
            inv = jnp.where(ok, 1.0 / jnp.where(ok, l_ref[rows, :], 1.0), 0.0)
            o_ref[0, :, h * dv:(h + 1) * dv] = (acc_ref[rows, :] * jnp.tile(inv, (1, dv // LANES))).astype(o_ref.dtype)


def flash(qs, ks, v, *, H, G=1, dks, dv, scale, q_cols=None, k_cols=None, v_col=0, window=0,
          cq=None, ck=None, bm=None, tq=128, tk=512, out_dtype=F32):
    B, S = qs[0].shape[:2]
    n = len(dks)
    q_cols = q_cols or [0] * n
    k_cols = k_cols or [0] * n
    tq = min(tq, S)
    tk = min(tk, S)
    assert tk % LANES == 0 and dv % LANES == 0
    nq = S // tq
    pairs = [(i, t) for i in range(nq) for t in range(_kv_lo(i, tq, tk, window), _kv_hi(i, tq, tk) + 1)]
    nsteps = len(pairs)
    tab = np.array([[i for i, _ in pairs], [t for _, t in pairs],
                    [int(t == _kv_lo(i, tq, tk, window)) for i, t in pairs],
                    [int(t == _kv_hi(i, tq, tk)) for i, t in pairs]], np.int32).reshape(-1)

    def qblk(s, tab):
        return tab[s]

    def kblk(s, tab):
        return tab[nsteps + s]

    in_specs, args = [], []
    for a, dk, c in zip(qs, dks, q_cols):
        in_specs.append(pl.BlockSpec((1, tq, H * dk), lambda b, g, s, tab, c=c: (b, qblk(s, tab), c + g)))
        args.append(a)
    for a, dk, c in zip(ks, dks, k_cols):
        in_specs.append(pl.BlockSpec((1, tk, dk), lambda b, g, s, tab, c=c: (b, kblk(s, tab), c + g)))
        args.append(a)
    if v is not None:
        in_specs.append(pl.BlockSpec((1, tk, dv), lambda b, g, s, tab: (b, kblk(s, tab), v_col + g)))
        args.append(v)
    if cq is not None:
        in_specs.append(pl.BlockSpec((1, 1, tq, H), lambda b, g, s, tab: (b, g, qblk(s, tab), 0)))
        in_specs.append(pl.BlockSpec((1, 1, H, tk), lambda b, g, s, tab: (b, g, 0, kblk(s, tab))))
        args += [cq, ck]
    if bm is not None:
        in_specs.append(pl.BlockSpec((1, tq, bm.shape[-1]), lambda b, g, s, tab: (b, qblk(s, tab), 0)))
        args.append(bm)
    R = H * tq
    scratch = [pltpu.VMEM((R, dk), BF16) for dk in dks]
    scratch += [pltpu.VMEM((R, tk), F32), pltpu.VMEM((R, tk), BF16),
                pltpu.VMEM((R, LANES), F32), pltpu.VMEM((R, LANES), F32), pltpu.VMEM((R, LANES), F32),
                pltpu.VMEM((R, dv), F32), pltpu.VMEM((tq, tk), F32)]
    if cq is not None:
        scratch.append(pltpu.VMEM((R, LANES), F32))
    return pl.pallas_call(
        functools.partial(_flash_body, H=H, tq=tq, tk=tk, dks=tuple(dks), dv=dv, c2=scale * LOG2E,
                          window=window, has_bias=cq is not None, has_bm=bm is not None,
                          v_is_k0=v is None, nsteps=nsteps),
        grid_spec=pltpu.PrefetchScalarGridSpec(
            num_scalar_prefetch=1, grid=(B, G, nsteps), in_specs=in_specs,
            out_specs=pl.BlockSpec((1, tq, H * dv), lambda b, g, s, tab: (b, qblk(s, tab), g)),
            scratch_shapes=scratch),
        out_shape=jax.ShapeDtypeStruct((B, S, G * H * dv), out_dtype),
        compiler_params=_cparams("parallel", "parallel", "arbitrary"),
        name="flash",
    )(jnp.asarray(tab), *args)


def _cmp_topk_body(q_ref, km_ref, vm_ref, oc_ref, bm_ref, p_ref, *, H, tq, ncp, pos0, tvalid, stacked):
    qi = pl.program_id(1)
    if stacked:
        qs = q_ref[0].astype(BF16)
    else:
        qs = jnp.concatenate([q_ref[0, :, h * NSA_DH:(h + 1) * NSA_DH] for h in range(H)],
                             axis=0).astype(BF16)
    s_full = _dot_nt(qs, km_ref[0].astype(BF16))
    r = lax.broadcasted_iota(jnp.int32, (tq, ncp), 0)
    c = lax.broadcasted_iota(jnp.int32, (tq, ncp), 1)
    if tvalid is None:
        pos = pos0 + qi * tq + r
    else:
        pos = pos0 + jnp.minimum(r, tvalid - 1)
    cmask = (c + 1) * NSA_CMP - 1 <= pos
    imp = jnp.zeros((tq, ncp), F32)
    for h in range(H):
        rows = slice(h * tq, (h + 1) * tq)
        s = jnp.where(cmask, s_full[rows] * NSA_SCALE, NEG)
        m = jnp.max(s, axis=-1, keepdims=True)
        p = jnp.where(cmask, jnp.exp(s - m), 0.0)
        p = p / jnp.maximum(jnp.sum(p, axis=-1, keepdims=True), 1e-30)
        imp = imp + p
        p_ref[rows] = p.astype(BF16)
    o = jnp.dot(p_ref[...], vm_ref[0].astype(BF16), preferred_element_type=F32)
    if stacked:
        oc_ref[0] = o.astype(oc_ref.dtype)
    else:
        for h in range(H):
            oc_ref[0, :, h * NSA_DH:(h + 1) * NSA_DH] = o[h * tq:(h + 1) * tq].astype(oc_ref.dtype)
    nxt = pltpu.roll(imp, ncp - 1, axis=1)
    prv = pltpu.roll(imp, 1, axis=1)
    ps = imp + jnp.where((c & 1) == 0, nxt, prv)
    blk = lax.shift_right_logical(c, 1)
    avail = blk * NSA_SEL <= pos
    cur = lax.shift_right_logical(pos, SEL_SHIFT)
    forced = (blk == 0) | (blk == cur) | (blk == cur - 1)
    sc = jnp.where(avail, ps + jnp.where(forced, NSA_FORCE, 0.0), NEG)
    rank = jnp.zeros((tq, ncp), F32)
    for i in range(ncp // 2):
        col = sc[:, 2 * i:2 * i + 1]
        beats = (col > sc) | ((col == sc) & (blk > i))
        rank = rank + jnp.where(beats, 1.0, 0.0)
    bm_ref[0] = jnp.where(avail & (rank < NSA_TOPK), 1.0, 0.0)


def cmp_topk(q, kmean, vmean, *, tq, pos0, tvalid=None, stacked=False):
    B = q.shape[0]
    ncp = kmean.shape[1]
    H = NSA_H
    if stacked:
        nq = 1
        q_spec = pl.BlockSpec((1, H * tq, NSA_DH), lambda b, i: (b, 0, 0))
        o_spec = pl.BlockSpec((1, H * tq, NSA_DH), lambda b, i: (b, 0, 0))
        o_shape = jax.ShapeDtypeStruct((B, H * tq, NSA_DH), F32)
    else:
        nq = q.shape[1] // tq
        q_spec = pl.BlockSpec((1, tq, H * NSA_DH), lambda b, i: (b, i, 0))
        o_spec = pl.BlockSpec((1, tq, H * NSA_DH), lambda b, i: (b, i, 0))
        o_shape = jax.ShapeDtypeStruct((B, nq * tq, H * NSA_DH), F32)
    return pl.pallas_call(
        functools.partial(_cmp_topk_body, H=H, tq=tq, ncp=ncp, pos0=pos0, tvalid=tvalid, stacked=stacked),
        grid=(B, nq),
        in_specs=[q_spec,
                  pl.BlockSpec((1, ncp, NSA_DH), lambda b, i: (b, 0, 0)),
                  pl.BlockSpec((1, ncp, NSA_DH), lambda b, i: (b, 0, 0))],
        out_specs=[o_spec, pl.BlockSpec((1, tq, ncp), lambda b, i: (b, i, 0))],
        out_shape=[o_shape, jax.ShapeDtypeStruct((B, nq * tq, ncp), F32)],
        scratch_shapes=[pltpu.VMEM((H * tq, ncp), BF16)],
        compiler_params=_cparams("parallel", "arbitrary"),
        name="cmp_topk",
    )(q, kmean, vmean)


def _block_mean_body(x_ref, o_ref, *, rows):
    x = x_ref[0].astype(F32)
    o_ref[0] = jnp.sum(x.reshape(rows // NSA_CMP, NSA_CMP, x.shape[-1]), axis=1) * (1.0 / NSA_CMP)


def block_mean(x, *, tr=1024):
    B, S, d = x.shape
    tr = min(tr, S)
    assert S % tr == 0 and (tr // NSA_CMP) % SUBLANES == 0
    return pl.pallas_call(
        functools.partial(_block_mean_body, rows=tr),
        grid=(B, S // tr),
        in_specs=[pl.BlockSpec((1, tr, d), lambda b, i: (b, i, 0))],
        out_specs=pl.BlockSpec((1, tr // NSA_CMP, d), lambda b, i: (b, i, 0)),
        out_shape=jax.ShapeDtypeStruct((B, S // NSA_CMP, d), F32),
        compiler_params=_cparams("parallel", "arbitrary"),
        name="block_mean",
    )(x)


def _pool_mean_body(pt_ref, *refs, pp):
    o_ref = refs[-1]
    per = PAGE // NSA_CMP
    for i in range(pp):
        x = refs[i][...].astype(F32)
        o_ref[0, i * per:(i + 1) * per, :] = jnp.sum(x.reshape(per, NSA_CMP, x.shape[-1]), axis=1) * (1.0 / NSA_CMP)


def pool_block_mean(pool, layer, pt_flat, db, npages, *, pp=32):
    d = pool.shape[-1]
    pp = min(pp, npages)
    assert npages % pp == 0
    per = PAGE // NSA_CMP
    in_specs = [pl.BlockSpec((None, None, PAGE, d),
                             lambda b, c, pt, i=i: (layer, pt[b * npages + c * pp + i], 0, 0))
                for i in range(pp)]
    return pl.pallas_call(
        functools.partial(_pool_mean_body, pp=pp),
        grid_spec=pltpu.PrefetchScalarGridSpec(
            num_scalar_prefetch=1, grid=(db, npages // pp), in_specs=in_specs,
            out_specs=pl.BlockSpec((1, pp * per, d), lambda b, c, pt: (b, c, 0))),
        out_shape=jax.ShapeDtypeStruct((db, npages * per, d), F32),
        compiler_params=_cparams("parallel", "arbitrary"),
        name="pool_block_mean",
    )(pt_flat, *([pool] * pp))


def _cumsum_body(x_ref, c_ref, ct_ref, carry_ref, *, blk):
    x = x_ref[0].astype(F32)
    nh = x.shape[-1]

    @pl.when(pl.program_id(1) == 0)
    def _():
        carry_ref[...] = jnp.zeros(carry_ref.shape, F32)

    r = lax.broadcasted_iota(jnp.int32, (blk, blk), 0)
    cc = lax.broadcasted_iota(jnp.int32, (blk, blk), 1)
    tri = jnp.where(cc <= r, 1.0, 0.0)
    c = jnp.dot(tri, x, preferred_element_type=F32, precision=lax.Precision.HIGHEST) + carry_ref[...]
    carry_ref[...] = c[blk - 1:blk, :]
    c_ref[0] = c
    eye = jnp.where(lax.broadcasted_iota(jnp.int32, (nh, nh), 0) == lax.broadcasted_iota(jnp.int32, (nh, nh), 1),
                    1.0, 0.0)
    ct_ref[0] = lax.dot_general(eye, c, (((1,), (1,)), ((), ())), preferred_element_type=F32,
                                precision=lax.Precision.HIGHEST)


def cumsum_seq(x):
    B, S, nh = x.shape
    blk = min(PAGE, S)
    return pl.pallas_call(
        functools.partial(_cumsum_body, blk=blk),
        grid=(B, S // blk),
        in_specs=[pl.BlockSpec((1, blk, nh), lambda b, i: (b, i, 0))],
        out_specs=[pl.BlockSpec((1, blk, nh), lambda b, i: (b, i, 0)),
                   pl.BlockSpec((1, nh, blk), lambda b, i: (b, 0, i))],
        out_shape=[jax.ShapeDtypeStruct((B, S, nh), F32), jax.ShapeDtypeStruct((B, nh, S), F32)],
        scratch_shapes=[pltpu.VMEM((1, nh), F32)],
        compiler_params=_cparams("parallel", "arbitrary"),
        name="cumsum_seq",
    )(x)


def _cumsum_pages_body(pt_ref, *refs, npages):
    o_ref = refs[-1]
    nh = o_ref.shape[1]
    r = lax.broadcasted_iota(jnp.int32, (PAGE, PAGE), 0)
    cc = lax.broadcasted_iota(jnp.int32, (PAGE, PAGE), 1)
    tri = jnp.where(r <= cc, 1.0, 0.0)
    x = jnp.concatenate([refs[j][...].astype(F32) for j in range(npages)], axis=0)
    local = jnp.dot(x, tri, preferred_element_type=F32, precision=lax.Precision.HIGHEST)
    carry = jnp.zeros((nh, 1), F32)
    for j in range(npages):
        cj = local[j * nh:(j + 1) * nh, :]
        o_ref[0, :, j * PAGE:(j + 1) * PAGE] = cj + carry
        carry = carry + cj[:, PAGE - 1:PAGE]


def cumsum_pages(pool_t, layer, pt_flat, db, npages):
    nh = pool_t.shape[2]
    in_specs = [pl.BlockSpec((None, None, nh, PAGE), lambda b, pt, j=j: (layer, pt[b * npages + j], 0, 0))
                for j in range(npages)]
    return pl.pallas_call(
        functools.partial(_cumsum_pages_body, npages=npages),
        grid_spec=pltpu.PrefetchScalarGridSpec(
            num_scalar_prefetch=1, grid=(db,), in_specs=in_specs,
            out_specs=pl.BlockSpec((1, nh, npages * PAGE), lambda b, pt: (b, 0, 0))),
        out_shape=jax.ShapeDtypeStruct((db, nh, npages * PAGE), F32),
        compiler_params=_cparams("parallel"),
        name="cumsum_pages",
    )(pt_flat, *([pool_t] * npages))


def _decode_body(pt_ref, *refs, H, G, dks, ktr, pp, dv, c2, window, has_bias, has_bm, v_is_k0,
                 nchunks, kpos0, qpos0, tvalid):
    nparts = len(dks)
    it = iter(refs)
    q_refs = [next(it) for _ in range(nparts)]
    kp_refs = [[next(it) for _ in range(pp)] for _ in range(nparts)]
    vp_refs = kp_refs[0] if v_is_k0 else [next(it) for _ in range(pp)]
    kn_refs = [next(it) for _ in range(nparts)]
    vn_ref = kn_refs[0] if v_is_k0 else next(it)
    cq_ref = ck_ref = ckn_ref = bm_ref = bmn_ref = None
    if has_bias:
        cq_ref, ck_ref, ckn_ref = next(it), next(it), next(it)
    if has_bm:
        bm_ref, bmn_ref = next(it), next(it)
    o_ref = next(it)
    m_ref, l_ref, acc_ref = next(it), next(it), next(it)
    R = H * TPAD
    c = pl.program_id(1)

    @pl.when(c == 0)
    def _init():
        m_ref[...] = jnp.full(m_ref.shape, NEG, F32)
        l_ref[...] = jnp.zeros(l_ref.shape, F32)
        acc_ref[...] = jnp.zeros(acc_ref.shape, F32)

    def page_rows(ref, g):
        return ref[...] if G == 1 else ref[pl.ds(g, PAGE, stride=G), :]

    GH = G * H
    qbs = [[qr[0, g].astype(BF16) for qr in q_refs] for g in range(G)]

    def rows_cat(parts):
        return parts[0] if G == 1 else jnp.concatenate(parts, axis=0)

    def update(s, kpos, ck, key_ok, causal, pv):
        K = s.shape[-1]
        t3 = (s * c2).reshape(GH, TPAD, K)
        if has_bias:
            t3 = t3 + ((cq_ref[0] * LOG2E).reshape(GH, TPAD, 1) - (ck * LOG2E)[:, None, :])
        mask = key_ok
        if causal or window:
            tok = jnp.minimum(lax.broadcasted_iota(jnp.int32, (TPAD, K), 0), tvalid - 1)
            qpos = qpos0 + tok
            if causal:
                cm = kpos <= qpos
                mask = cm if mask is None else mask & cm
            if window:
                wm = (qpos - kpos) < window
                mask = wm if mask is None else mask & wm
        if mask is not None:
            t3 = jnp.where(mask[None], t3, NEG)
        t = t3.reshape(G * R, K)
        m_prev = m_ref[...]
        m_new = jnp.maximum(m_prev, jnp.max(t, axis=-1, keepdims=True))
        alpha = jnp.exp2(m_prev - m_new)
        p = jnp.exp2(t - m_new)
        l_ref[...] = alpha * l_ref[...] + jnp.sum(p, axis=-1, keepdims=True)
        m_ref[...] = m_new
        acc_ref[...] = acc_ref[...] * alpha + pv(p.astype(BF16))

    kb0 = [[] for _ in range(G)]
    s_groups = []
    for g in range(G):
        s_list = []
        for j in range(pp):
            s = None
            for i in range(nparts):
                if ktr[i]:
                    d = jnp.dot(qbs[g][i], kp_refs[i][j][...].astype(BF16), preferred_element_type=F32)
                else:
                    kb = page_rows(kp_refs[i][j], g).astype(BF16)
                    if i == 0:
                        kb0[g].append(kb)
                    d = _dot_nt(qbs[g][i], kb)
                s = d if s is None else s + d
            s_list.append(s)
        s_groups.append(jnp.concatenate(s_list, axis=1) if pp > 1 else s_list[0])
    K = pp * PAGE
    kpos = kpos0 + c * K + lax.broadcasted_iota(jnp.int32, (TPAD, K), 1)

    def pv_past(pb):
        outs = []
        for g in range(G):
            out = None
            for j in range(pp):
                vb = kb0[g][j] if v_is_k0 else page_rows(vp_refs[j], g).astype(BF16)
                d = jnp.dot(pb[g * R:(g + 1) * R, j * PAGE:(j + 1) * PAGE], vb, preferred_element_type=F32)
                out = d if out is None else out + d
            outs.append(out)
        return rows_cat(outs)

    sel_ok = None
    if has_bm:
        bmc = bm_ref[0]
        in_first = lax.broadcasted_iota(jnp.int32, (TPAD, PAGE), 1) < NSA_SEL
        sel_ok = jnp.concatenate(
            [jnp.where(in_first, bmc[:, LPP * j:LPP * j + 1], bmc[:, LPP * j + LPB:LPP * j + LPB + 1])
             for j in range(pp)], axis=1) > 0.5

    update(rows_cat(s_groups), kpos, ck_ref[0].reshape(GH, K) if has_bias else None, sel_ok, False, pv_past)

    @pl.when(c == nchunks - 1)
    def _new_and_fin():
        s_groups, vnbs = [], []
        for g in range(G):
            knb = [kn[0, :, g * dk:(g + 1) * dk].astype(BF16) for kn, dk in zip(kn_refs, dks)]
            s = None
            for i in range(nparts):
                d = _dot_nt(qbs[g][i], knb[i])
                s = d if s is None else s + d
            s_groups.append(s)
            vnbs.append(knb[0] if v_is_k0 else vn_ref[0, :, g * dv:(g + 1) * dv].astype(BF16))
        j = lax.broadcasted_iota(jnp.int32, (TPAD, TPAD), 1)

        def pv_new(pb):
            return rows_cat([jnp.dot(pb[g * R:(g + 1) * R], vnbs[g], preferred_element_type=F32)
                             for g in range(G)])

        new_ok = j < tvalid
        if has_bm:
            new_ok = new_ok & (bmn_ref[0] > 0.5)
        update(rows_cat(s_groups), qpos0 + j, ckn_ref[0].reshape(GH, TPAD) if has_bias else None,
               new_ok, True, pv_new)
        ok = m_ref[...] > 0.5 * NEG
        o = jnp.where(ok, acc_ref[...] / jnp.where(ok, l_ref[...], 1.0), 0.0)
        o_ref[0] = o.reshape(G, R, dv).astype(o_ref.dtype)


def decode(qs, pools, vpool, layer, pt_flat, npages, knews, vnew, *, H, G=1, dks, dv, scale,
           qpos0, kpos0=0, tvalid, ktr=None, window=0, cq=None, ck=None, cknew=None, bm=None, pp=32):
    DB = qs[0].shape[0]
    nparts = len(dks)
    ktr = ktr or (False,) * nparts
    pp = min(pp, npages)
    assert npages % pp == 0
    nchunks = npages // pp
    R = H * TPAD
    in_specs, args = [], []
    for a, dk in zip(qs, dks):
        in_specs.append(pl.BlockSpec((1, G, R, dk), lambda b, c, pt: (b, 0, 0, 0)))
        args.append(a)

    def page_specs(shape):
        return [pl.BlockSpec((None, None) + shape,
                             lambda b, c, pt, j=j: (layer, pt[b * npages + c * pp + j], 0, 0))
                for j in range(pp)]

    for a, dk, tr in zip(pools, dks, ktr):
        in_specs += page_specs((dk, PAGE) if tr else (PAGE * G, dk))
        args += [a] * pp
    if vpool is not None:
        in_specs += page_specs((PAGE * G, dv))
        args += [vpool] * pp
    for a, dk in zip(knews, dks):
        in_specs.append(pl.BlockSpec((1, TPAD, G * dk), lambda b, c, pt: (b, 0, 0)))
        args.append(a)
    if vnew is not None:
        in_specs.append(pl.BlockSpec((1, TPAD, G * dv), lambda b, c, pt: (b, 0, 0)))
        args.append(vnew)
    if cq is not None:
        in_specs.append(pl.BlockSpec((1, G, R, 1), lambda b, c, pt: (b, 0, 0, 0)))
        in_specs.append(pl.BlockSpec((1, G, H, pp * PAGE), lambda b, c, pt: (b, 0, 0, c)))
        in_specs.append(pl.BlockSpec((1, G, H, TPAD), lambda b, c, pt: (b, 0, 0, 0)))
        args += [cq, ck, cknew]
    if bm is not None:
        lpc = pp * LPP
        assert tvalid <= NSA_SEL and qpos0 % NSA_SEL == 0 and kpos0 == 0
        if nchunks == 1:
            in_specs.append(pl.BlockSpec((1, TPAD, bm.shape[-1]), lambda b, c, pt: (b, 0, 0)))
        else:
            assert lpc % LANES == 0
            in_specs.append(pl.BlockSpec((1, TPAD, lpc), lambda b, c, pt: (b, 0, c)))
        new_lane = LPB * (qpos0 // NSA_SEL)
        in_specs.append(pl.BlockSpec((1, TPAD, 1), lambda b, c, pt: (b, 0, 0)))
        args += [bm, bm[:, :, new_lane:new_lane + 1]]
    return pl.pallas_call(
        functools.partial(_decode_body, H=H, G=G, dks=tuple(dks), ktr=tuple(ktr), pp=pp, dv=dv,
                          c2=scale * LOG2E, window=window, has_bias=cq is not None, has_bm=bm is not None,
                          v_is_k0=vpool is None, nchunks=nchunks, kpos0=kpos0, qpos0=qpos0, tvalid=tvalid),
        grid_spec=pltpu.PrefetchScalarGridSpec(
            num_scalar_prefetch=1, grid=(DB, nchunks), in_specs=in_specs,
            out_specs=pl.BlockSpec((1, G, R, dv), lambda b, c, pt: (b, 0, 0, 0)),
            scratch_shapes=[pltpu.VMEM((G * R, 1), F32), pltpu.VMEM((G * R, 1), F32),
                            pltpu.VMEM((G * R, dv), F32)]),
        out_shape=jax.ShapeDtypeStruct((DB, G, R, dv), F32),
        compiler_params=_cparams("parallel", "arbitrary"),
        name="decode",
    )(pt_flat, *args)


def _nsa_mix_body(oc_ref, os_ref, ow_ref, gt_ref, g_ref, o_ref):
    gt = gt_ref[...]
    for h in range(NSA_H):
        cols = slice(h * NSA_DH, (h + 1) * NSA_DH)
        o = (gt[:, h:h + 1] * oc_ref[:, cols] + gt[:, NSA_H + h:NSA_H + h + 1] * os_ref[:, cols]
             + gt[:, 2 * NSA_H + h:2 * NSA_H + h + 1] * ow_ref[:, cols])
        g = g_ref[:, cols]
        o_ref[:, cols] = (o * (g * jax.nn.sigmoid(g))).astype(o_ref.dtype)


def nsa_mix(oc, osel, ow, gates, proj, g_col, *, tm=256):
    M, W = oc.shape
    tm = min(tm, M)
    assert M % tm == 0
    spec = pl.BlockSpec((tm, W), lambda i: (i, 0))
    return pl.pallas_call(
        _nsa_mix_body,
        grid=(M // tm,),
        in_specs=[spec, spec, spec, pl.BlockSpec((tm, gates.shape[1]), lambda i: (i, 0)),
                  pl.BlockSpec((tm, W), lambda i: (i, g_col))],
        out_specs=spec,
        out_shape=jax.ShapeDtypeStruct((M, W), BF16),
        compiler_params=_cparams("parallel"),
        name="nsa_mix",
    )(oc, osel, ow, gates, proj)


def _shift_append_body(buf_ref, new_ref, o_ref, *, wb, t):
    o_ref[:, 0:wb - t, :] = buf_ref[:, t:wb, :]
    o_ref[:, wb - t:wb, :] = new_ref[:, 0:t, :]


def shift_append(state, layer, new, t, *, tb=8):
    _, DB, wb, d = state.shape
    tb = min(tb, DB)
    assert DB % tb == 0
    return pl.pallas_call(
        functools.partial(_shift_append_body, wb=wb, t=t),
        grid=(DB // tb,),
        in_specs=[pl.BlockSpec((None, tb, wb, d), lambda i: (layer, i, 0, 0)),
                  pl.BlockSpec((tb, TPAD, d), lambda i: (i, 0, 0))],
        out_specs=pl.BlockSpec((tb, wb, d), lambda i: (i, 0, 0)),
        out_shape=jax.ShapeDtypeStruct((DB, wb, d), state.dtype),
        compiler_params=_cparams("parallel"),
        name="shift_append",
    )(state, new)


def _rope(x, pos):
    half = x.shape[-1] // 2
    inv = ROPE_THETA ** (-jnp.arange(half, dtype=F32) / half)
    ang = pos.astype(F32)[:, None] * inv[None, :]
    cos = jnp.cos(ang)[:, None, :]
    sin = jnp.sin(ang)[:, None, :]
    x1, x2 = x[..., :half], x[..., half:]
    return jnp.concatenate([x1 * cos - x2 * sin, x1 * sin + x2 * cos], axis=-1)


def _pad_rows(x, n, axis):
    pad = [(0, 0)] * x.ndim
    pad[axis] = (0, n - x.shape[axis])
    return jnp.pad(x, pad)


def _stack_rows(x, H, d):
    DB, T, _ = x.shape
    x = x.reshape(DB, T, H, d).transpose(0, 2, 1, 3)
    return _pad_rows(x, TPAD, 2).reshape(DB, 1, H * TPAD, d)


def _unstack_rows(o, H, T):
    DB, _, d = o.shape
    return o.reshape(DB, H, TPAD, d)[:, :, :T].transpose(0, 2, 1, 3).reshape(DB, T, H * d)


def _even_weights(w_in, w_uq, w_uk, w_uv, w_out):
    D = w_in.shape[0]
    o = np.cumsum((0, MLA_QL, MLA_KVL, MLA_ROPE, MLA_H * MLA_V, FOX_H * FOX_DH, FOX_KVH * FOX_DH,
                   FOX_KVH * FOX_DH, FOX_H, FOX_H * FOX_DH))
    seg = lambda i: w_in[:, o[i]:o[i + 1]]
    order = [seg(4), seg(3), seg(8), seg(0), seg(1), seg(5), seg(6), seg(2), seg(7)]
    n = sum(s.shape[1] for s in order)
    w = jnp.concatenate(order + [jnp.zeros((D, _round_up(n, 512) - n), w_in.dtype)], axis=1).astype(BF16)
    uq = jnp.concatenate([w_uq[:, :, :MLA_NOPE].reshape(MLA_QL, -1), w_uq[:, :, MLA_NOPE:].reshape(MLA_QL, -1)],
                         axis=1).astype(BF16)
    uk = w_uk.transpose(1, 2, 0).astype(BF16)
    uv = w_uv.transpose(1, 0, 2).astype(BF16)
    return w, uq, uk, uv, w_out.astype(BF16)


EV_COL = dict(fq=0, gm=1024, gf=2048, cq=3072, ckv=3584, fk=3840, fv=4096, kr=4352, fl=4416)


def _odd_weights(w_in, gate_bias, w_out):
    D = w_in.shape[0]
    W = NSA_H * NSA_DH
    q, kv, gl, g = w_in[:, :W], w_in[:, W:W + 6 * NSA_DH], w_in[:, W + 6 * NSA_DH:W + 6 * NSA_DH + 3 * NSA_H], \
        w_in[:, W + 6 * NSA_DH + 3 * NSA_H:]
    gl = gl.reshape(D, NSA_H, 3).transpose(0, 2, 1).reshape(D, 3 * NSA_H)
    n = 2 * W + 6 * NSA_DH + 3 * NSA_H
    w = jnp.concatenate([q, g, kv, gl, jnp.zeros((D, _round_up(n, 512) - n), w_in.dtype)], axis=1).astype(BF16)
    gb = gate_bias.reshape(NSA_H, 3).T.reshape(3 * NSA_H)
    return w, gb, w_out.astype(BF16)


def kernel(x_prompt, x_sample, cache_mla_ckv, cache_mla_krope, cache_fox_k, cache_fox_v, cache_fox_logf, cache_nsa_cmp_k, cache_nsa_cmp_v, cache_nsa_sel_k, cache_nsa_sel_v, state_nsa_win_k, state_nsa_win_v, page_table, c_prompt, c_sample, ada_w, ada_b, norm_g, final_g, ev_w_in, mla_q_norm, mla_kv_norm, mla_w_uq, mla_w_uk, mla_w_uv, fox_f_bias, ev_w_out, od_w_in, nsa_gate_bias, od_w_out):
    B, S, D = x_prompt.shape
    DB, T, _ = x_sample.shape
    depth = ada_w.shape[0]
    npages = page_table.shape[1]
    P = npages * PAGE
    MS = DB * T
    pos_p = jnp.arange(S)
    pos_s = P + jnp.arange(T)
    pt_flat = page_table.reshape(-1).astype(jnp.int32)

    c_all = jnp.concatenate([c_prompt, c_sample], axis=0)
    mc = _round_up(B + DB, SUBLANES)
    c_all = _pad_rows(c_all, mc, 0)[None]
    fox_pool_k = cache_fox_k.reshape(cache_fox_k.shape[:2] + (PAGE * FOX_KVH, FOX_DH))
    fox_pool_v = cache_fox_v.reshape(cache_fox_v.shape[:2] + (PAGE * FOX_KVH, FOX_DH))
    krope_pool_t = jnp.swapaxes(cache_mla_krope, 2, 3)
    logf_pool_t = jnp.swapaxes(cache_fox_logf, 2, 3)

    xp = x_prompt
    xs = x_sample.reshape(1, MS, D)
    outs = {k: [] for k in ('ckv_p', 'ckv_s', 'kr_p', 'kr_s', 'fk_p', 'fk_s', 'fv_p', 'fv_s', 'lf_p', 'lf_s',
                            'kc_p', 'kc_s', 'vc_p', 'vc_s', 'ks_p', 'ks_s', 'vs_p', 'vs_s',
                            'wk_p', 'wk_s', 'wv_p', 'wv_s')}

    for l in range(depth):
        mod = mm(c_all, ada_w, w_idx=l, bias=ada_b[l][None])[0]
        shift, scale, gate = mod[:, :D], mod[:, D:2 * D], mod[:, 2 * D:]
        rep = lambda a: jnp.repeat(a[B:B + DB], T, axis=0)[None]
        hp = norm(xp, norm_g[l], scale[:B, None], shift[:B, None], out_dtype=BF16)
        hs = norm(xs, norm_g[l], rep(scale), rep(shift), out_dtype=BF16)
        gate_p, gate_s = gate[:B, None], rep(gate)

        if l % 2 == 0:
            e = l // 2
            w_in, w_uq, w_uk, w_uv, w_out = _even_weights(ev_w_in[e], mla_w_uq[e], mla_w_uk[e], mla_w_uv[e],
                                                          ev_w_out[e])
            C = EV_COL

            def project(h, pos):
                proj = mm(h, w_in)
                g_, r_, _ = proj.shape
                cqn = norm(proj, mla_q_norm[e], col=C['cq'] // MLA_QL, width=MLA_QL, out_dtype=BF16)
                q = mm(cqn, w_uq)
                q_lat = mm_heads(q.reshape(g_ * r_, -1), w_uk).reshape(g_, r_, -1)
                q_rope = _rope(q[..., MLA_H * MLA_NOPE:].reshape(g_, r_, MLA_H, MLA_ROPE), pos).reshape(g_, r_, -1)
                ckv = norm(proj, mla_kv_norm[e], col=C['ckv'] // MLA_KVL, width=MLA_KVL)
                kr = _rope(proj[..., None, C['kr']:C['kr'] + MLA_ROPE], pos)[..., 0, :]
                logf = jax.nn.log_sigmoid(proj[..., C['fl']:C['fl'] + FOX_H] + fox_f_bias[e])
                return proj, q_lat, q_rope, ckv, kr, logf

            def output(proj, o_lat, o_fox, x, gate_):
                g_, r_, _ = proj.shape
                o_mla = mm_heads(o_lat.reshape(g_ * r_, -1), w_uv).reshape(g_, r_, -1)
                gm = proj[..., C['gm']:C['gm'] + MLA_H * MLA_V]
                gf = proj[..., C['gf']:C['gf'] + FOX_H * FOX_DH]
                mix = jnp.concatenate([o_mla * jax.nn.silu(gm), o_fox * jax.nn.silu(gf)], axis=-1).astype(BF16)
                return mm(mix, w_out, res=x, gate=gate_)

            proj, q_lat, q_rope, ckv, kr, logf = project(hp, pos_p)
            o_lat = flash([q_lat, q_rope], [ckv, kr], None, H=MLA_H, dks=(MLA_KVL, MLA_ROPE), dv=MLA_KVL,
                          scale=MLA_SCALE, tq=256)
            csum, csum_t = cumsum_seq(logf)
            cq = csum.reshape(B, S, FOX_KVH, FOX_G).transpose(0, 2, 1, 3)
            ck = csum_t.reshape(B, FOX_KVH, FOX_G, S)
            kb = C['fk'] // FOX_DH
            vb = C['fv'] // FOX_DH
            o_fox = flash([proj], [proj], proj, H=FOX_G, G=FOX_KVH, dks=(FOX_DH,), dv=FOX_DH, scale=FOX_SCALE,
                          q_cols=[0], k_cols=[kb], v_col=vb, cq=cq, ck=ck, tq=256)
            fk = proj[..., C['fk']:C['fk'] + FOX_KVH * FOX_DH]
            fv = proj[..., C['fv']:C['fv'] + FOX_KVH * FOX_DH]
            outs['ckv_p'].append(ckv)
            outs['kr_p'].append(kr)
            outs['fk_p'].append(fk.reshape(B, S, FOX_KVH, FOX_DH))
            outs['fv_p'].append(fv.reshape(B, S, FOX_KVH, FOX_DH))
            outs['lf_p'].append(logf)
            xp = output(proj, o_lat, o_fox, xp, gate_p)

            proj, q_lat, q_rope, ckv, kr, logf = project(hs, jnp.tile(pos_s, DB))
            ckv3, kr3 = ckv.reshape(DB, T, -1), kr.reshape(DB, T, -1)
            o_lat = decode([_stack_rows(q_lat.reshape(DB, T, -1), MLA_H, MLA_KVL),
                            _stack_rows(q_rope.reshape(DB, T, -1), MLA_H, MLA_ROPE)],
                           [cache_mla_ckv, krope_pool_t], None, e, pt_flat, npages,
                           [_pad_rows(ckv3, TPAD, 1), _pad_rows(kr3, TPAD, 1)], None,
                           H=MLA_H, dks=(MLA_KVL, MLA_ROPE), ktr=(False, True), dv=MLA_KVL, scale=MLA_SCALE,
                           qpos0=P, tvalid=T)
            o_lat = _unstack_rows(o_lat[:, 0], MLA_H, T).reshape(1, MS, -1)
            cpast_t = cumsum_pages(logf_pool_t, e, pt_flat, DB, npages)
            lf3 = logf.reshape(DB, T, FOX_H)
            c_new = cpast_t[:, :, -1][:, None, :] + jnp.cumsum(lf3, axis=1)
            fq = proj[0, :, :FOX_H * FOX_DH].reshape(DB, T, FOX_KVH, FOX_G, FOX_DH).transpose(0, 2, 3, 1, 4)
            fq = _pad_rows(fq, TPAD, 3).reshape(DB, FOX_KVH, FOX_G * TPAD, FOX_DH)
            cqs = _pad_rows(c_new.reshape(DB, T, FOX_KVH, FOX_G).transpose(0, 2, 3, 1), TPAD, 3)
            fk = proj[0, :, C['fk']:C['fk'] + FOX_KVH * FOX_DH].reshape(DB, T, -1)
            fv = proj[0, :, C['fv']:C['fv'] + FOX_KVH * FOX_DH].reshape(DB, T, -1)
            o_fox = decode([fq], [fox_pool_k], fox_pool_v, e, pt_flat, npages,
                           [_pad_rows(fk, TPAD, 1)], _pad_rows(fv, TPAD, 1),
                           H=FOX_G, G=FOX_KVH, dks=(FOX_DH,), dv=FOX_DH, scale=FOX_SCALE, qpos0=P, tvalid=T,
                           cq=cqs.reshape(DB, FOX_KVH, FOX_G * TPAD, 1),
                           ck=cpast_t.reshape(DB, FOX_KVH, FOX_G, P), cknew=cqs)
            o_fox = o_fox.reshape(DB, FOX_KVH, FOX_G, TPAD, FOX_DH)[:, :, :, :T].transpose(0, 3, 1, 2, 4)
            o_fox = o_fox.reshape(1, MS, FOX_H * FOX_DH)
            outs['ckv_s'].append(ckv3)
            outs['kr_s'].append(kr3)
            outs['fk_s'].append(fk.reshape(DB, T, FOX_KVH, FOX_DH))
            outs['fv_s'].append(fv.reshape(DB, T, FOX_KVH, FOX_DH))
            outs['lf_s'].append(lf3)
            xs = output(proj, o_lat, o_fox, xs, gate_s)
        else:
            od = l // 2
            w_in, gbias, w_out = _odd_weights(od_w_in[od], nsa_gate_bias[od], od_w_out[od])
            W = NSA_H * NSA_DH
            KV0 = 2 * W

            def project(h, pos):
                proj = mm(h, w_in)
                g_, r_, _ = proj.shape
                q = _rope(proj[..., :W].reshape(g_, r_, NSA_H, NSA_DH), pos).reshape(g_, r_, W)
                kv = [proj[..., KV0 + i * NSA_DH:KV0 + (i + 1) * NSA_DH] for i in range(6)]
                kc, vc, ks, vs, kw, vw = kv
                kc, ks, kw = (_rope(k[..., None, :], pos)[..., 0, :] for k in (kc, ks, kw))
                gates = jax.nn.sigmoid(proj[..., KV0 + 6 * NSA_DH:KV0 + 6 * NSA_DH + 3 * NSA_H] + gbias)
                return proj, q, kc, vc, ks, vs, kw, vw, gates

            def output(proj, o_c, o_s, o_w, gates, x, gate_):
                g_, r_, _ = proj.shape
                mix = nsa_mix(o_c.reshape(g_ * r_, W), o_s.reshape(g_ * r_, W), o_w.reshape(g_ * r_, W),
                              gates.reshape(g_ * r_, -1), proj.reshape(g_ * r_, -1), 1)
                return mm(mix.reshape(g_, r_, W), w_out, res=x, gate=gate_)

            proj, q, kc, vc, ks, vs, kw, vw, gates = project(hp, pos_p)
            ncp = _round_up(S // NSA_CMP, LANES)
            kmean = _pad_rows(block_mean(kc, tr=min(1024, S)), ncp, 1)
            vmean = _pad_rows(block_mean(vc, tr=min(1024, S)), ncp, 1)
            tq = min(128, S)
            o_c, bm = cmp_topk(q, kmean, vmean, tq=tq, pos0=0)
            o_s = flash([q], [ks], vs, H=NSA_H, dks=(NSA_DH,), dv=NSA_DH, scale=NSA_SCALE, bm=bm, tq=256)
            o_w = flash([q], [kw], vw, H=NSA_H, dks=(NSA_DH,), dv=NSA_DH, scale=NSA_SCALE, window=NSA_WIN,
                        tq=tq, tk=tq)
            wbp = min(NSA_WIN, S)
            for n_, v_ in zip(('kc_p', 'vc_p', 'ks_p', 'vs_p', 'wk_p', 'wv_p'),
                              (kc, vc, ks, vs, kw[:, S - wbp:], vw[:, S - wbp:])):
                outs[n_].append(v_)
            xp = output(proj, o_c, o_s, o_w, gates, xp, gate_p)

            proj, q, kc, vc, ks, vs, kw, vw, gates = project(hs, jnp.tile(pos_s, DB))
            r3 = lambda a: a.reshape(DB, T, -1)
            kc, vc, ks, vs, kw, vw = map(r3, (kc, vc, ks, vs, kw, vw))
            qst = _stack_rows(r3(q), NSA_H, NSA_DH)
            ncp = _round_up((P + T + NSA_SEL - 1) // NSA_SEL * 2, LANES)
            kmean = _pad_rows(pool_block_mean(cache_nsa_cmp_k, od, pt_flat, DB, npages), ncp, 1)
            vmean = _pad_rows(pool_block_mean(cache_nsa_cmp_v, od, pt_flat, DB, npages), ncp, 1)
            o_c, bm = cmp_topk(qst[:, 0], kmean, vmean, tq=TPAD, pos0=P, tvalid=T, stacked=True)
            pad8 = lambda a: _pad_rows(a, TPAD, 1)
            o_s = decode([qst], [cache_nsa_sel_k], cache_nsa_sel_v, od, pt_flat, npages, [pad8(ks)], pad8(vs),
                         H=NSA_H, dks=(NSA_DH,), dv=NSA_DH, scale=NSA_SCALE, qpos0=P, tvalid=T, bm=bm)
            wb = state_nsa_win_k.shape[2]
            wpages = wb // PAGE
            wtab = jnp.arange(DB * wpages, dtype=jnp.int32)
            wshape = (state_nsa_win_k.shape[0], DB * wpages, PAGE, NSA_DH)
            o_w = decode([qst], [state_nsa_win_k.reshape(wshape)], state_nsa_win_v.reshape(wshape), od, wtab,
                         wpages, [pad8(kw)], pad8(vw), H=NSA_H, dks=(NSA_DH,), dv=NSA_DH, scale=NSA_SCALE,
                         qpos0=P, kpos0=P - wb, tvalid=T, window=NSA_WIN, pp=wpages)
            un = lambda o: _unstack_rows(o, NSA_H, T).reshape(1, MS, W)
            bk = shift_append(state_nsa_win_k, od, pad8(kw), T)
            bv = shift_append(state_nsa_win_v, od, pad8(vw), T)
            for n_, v_ in zip(('kc_s', 'vc_s', 'ks_s', 'vs_s', 'wk_s', 'wv_s'), (kc, vc, ks, vs, bk, bv)):
                outs[n_].append(v_)
            xs = output(proj, un(o_c), un(o_s[:, 0]), un(o_w[:, 0]), gates, xs, gate_s)

    y_prompt = norm(xp, final_g)
    y_sample = norm(xs, final_g).reshape(DB, T, D)
    st = {k: jnp.stack(v) for k, v in outs.items()}
    return (y_prompt, y_sample, st['ckv_p'], st['ckv_s'], st['kr_p'], st['kr_s'],
            st['fk_p'], st['fk_s'], st['fv_p'], st['fv_s'], st['lf_p'], st['lf_s'],
            st['kc_p'], st['kc_s'], st['vc_p'], st['vc_s'], st['ks_p'], st['ks_s'],
            st['vs_p'], st['vs_s'], st['wk_p'], st['wk_s'], st['wv_p'], st['wv_s'])
```

```python
import functools
import math

import numpy as np
import jax
import jax.numpy as jnp
from jax import lax
from jax.experimental import pallas as pl
from jax.experimental.pallas import tpu as pltpu

F32 = jnp.float32
BF16 = jnp.bfloat16
NEG = -1e30
LOG2E = math.log2(math.e)
VMEM_LIMIT = 48 * 1024 * 1024
LANES = 128
SUBLANES = 8

EPS = 1e-6
ROPE_THETA = 10000.0
PAGE = 128
MLA_H, MLA_NOPE, MLA_ROPE, MLA_V, MLA_QL, MLA_KVL = 8, 128, 64, 128, 512, 256
MLA_SCALE = (MLA_NOPE + MLA_ROPE) ** -0.5
FOX_H, FOX_KVH, FOX_DH = 8, 2, 128
FOX_G = FOX_H // FOX_KVH
FOX_SCALE = FOX_DH ** -0.5
NSA_H, NSA_DH, NSA_CMP, NSA_SEL, NSA_TOPK, NSA_WIN = 16, 128, 32, 64, 16, 512
NSA_FORCE = 1000.0
NSA_SCALE = NSA_DH ** -0.5
SEL_SHIFT = 6
LPP = PAGE // NSA_CMP
LPB = NSA_SEL // NSA_CMP
TPAD = SUBLANES
ROW_CHUNK = 16


def _cparams(*sem):
    return pltpu.CompilerParams(dimension_semantics=sem, vmem_limit_bytes=VMEM_LIMIT)


def _dot_nt(a, b):
    return lax.dot_general(a, b, (((1,), (1,)), ((), ())), preferred_element_type=F32)


def _round_up(n, m):
    return -(-n // m) * m


def _mm_body(x_ref, w_ref, *rest, has_bias, has_res):
    o_ref = rest[-1]
    acc = jnp.dot(x_ref[0].astype(BF16), w_ref[...].astype(BF16), preferred_element_type=F32)
    k = 0
    if has_bias:
        acc = acc + rest[k][...]
        k += 1
    if has_res:
        acc = rest[k][0] + rest[k + 1][0] * acc
    o_ref[0] = acc.astype(o_ref.dtype)


def mm(x, w, *, w_idx=None, bias=None, res=None, gate=None, out_dtype=F32, tm=1024, tn=512):
    G, R, K = x.shape
    N = w.shape[-1]
    tm = min(tm, R)
    tn = min(tn, N)
    assert R % tm == 0 and N % tn == 0, (R, tm, N, tn)
    in_specs = [pl.BlockSpec((1, tm, K), lambda g, i, j: (g, i, 0))]
    if w.ndim == 3:
        in_specs.append(pl.BlockSpec((None, K, tn), lambda g, i, j: (w_idx, 0, j)))
    else:
        in_specs.append(pl.BlockSpec((K, tn), lambda g, i, j: (0, j)))
    args = [x, w]
    if bias is not None:
        in_specs.append(pl.BlockSpec((1, tn), lambda g, i, j: (0, j)))
        args.append(bias)
    if res is not None:
        in_specs.append(pl.BlockSpec((1, tm, tn), lambda g, i, j: (g, i, j)))
        args.append(res)
        if gate.shape[1] == 1:
            in_specs.append(pl.BlockSpec((1, 1, tn), lambda g, i, j: (g, 0, j)))
        else:
            in_specs.append(pl.BlockSpec((1, tm, tn), lambda g, i, j: (g, i, j)))
        args.append(gate)
    return pl.pallas_call(
        functools.partial(_mm_body, has_bias=bias is not None, has_res=res is not None),
        grid=(G, R // tm, N // tn),
        in_specs=in_specs,
        out_specs=pl.BlockSpec((1, tm, tn), lambda g, i, j: (g, i, j)),
        out_shape=jax.ShapeDtypeStruct((G, R, N), out_dtype),
        compiler_params=_cparams("parallel", "parallel", "arbitrary"),
        name="mm",
    )(*args)


def _mm_heads_body(x_ref, w_ref, o_ref):
    o_ref[...] = jnp.dot(x_ref[...].astype(BF16), w_ref[0].astype(BF16),
                         preferred_element_type=F32).astype(o_ref.dtype)


def mm_heads(x, w, *, col_off=0, out_dtype=F32, tm=2048):
    M = x.shape[0]
    H, dk, dn = w.shape
    tm = min(tm, M)
    assert M % tm == 0
    return pl.pallas_call(
        _mm_heads_body,
        grid=(M // tm, H),
        in_specs=[pl.BlockSpec((tm, dk), lambda i, h: (i, col_off + h)),
                  pl.BlockSpec((1, dk, dn), lambda i, h: (h, 0, 0))],
        out_specs=pl.BlockSpec((tm, dn), lambda i, h: (i, h)),
        out_shape=jax.ShapeDtypeStruct((M, H * dn), out_dtype),
        compiler_params=_cparams("parallel", "arbitrary"),
        name="mm_heads",
    )(x, w)


def _norm_body(x_ref, g_ref, *rest, mod):
    o_ref = rest[-1]
    xf = x_ref[0].astype(F32)
    y = xf * lax.rsqrt(jnp.mean(xf * xf, axis=-1, keepdims=True) + EPS) * g_ref[...]
    if mod:
        y = y * (1.0 + rest[0][0]) + rest[1][0]
    o_ref[0] = y.astype(o_ref.dtype)


def norm(x, g, scale=None, shift=None, *, col=0, width=None, out_dtype=F32, tr=512):
    G, R, _ = x.shape
    width = width or x.shape[-1]
    tr = min(tr, R)
    assert R % tr == 0
    in_specs = [pl.BlockSpec((1, tr, width), lambda a, i: (a, i, col)),
                pl.BlockSpec((1, width), lambda a, i: (0, 0))]
    args = [x, g.reshape(1, width)]
    mod = scale is not None
    if mod:
        for arr in (scale, shift):
            if arr.shape[1] == 1:
                in_specs.append(pl.BlockSpec((1, 1, width), lambda a, i: (a, 0, 0)))
            else:
                in_specs.append(pl.BlockSpec((1, tr, width), lambda a, i: (a, i, 0)))
            args.append(arr)
    return pl.pallas_call(
        functools.partial(_norm_body, mod=mod),
        grid=(G, R // tr),
        in_specs=in_specs,
        out_specs=pl.BlockSpec((1, tr, width), lambda a, i: (a, i, 0)),
        out_shape=jax.ShapeDtypeStruct((G, R, width), out_dtype),
        compiler_params=_cparams("parallel", "arbitrary"),
        name="norm",
    )(*args)


def _kv_hi(qi, tq, tk):
    return (qi * tq + tq - 1) // tk


def _kv_lo(qi, tq, tk, window):
    return max(qi * tq - (window - 1), 0) // tk if window else 0


def _flash_body(tab_ref, *refs, H, tq, tk, dks, dv, c2, window, has_bias, has_bm, v_is_k0, nsteps):
    it = iter(refs)
    q_refs = [next(it) for _ in dks]
    k_refs = [next(it) for _ in dks]
    v_ref = None if v_is_k0 else next(it)
    cq_ref = ck_ref = bm_ref = None
    if has_bias:
        cq_ref, ck_ref = next(it), next(it)
    if has_bm:
        bm_ref = next(it)
    o_ref = next(it)
    qs_refs = [next(it) for _ in dks]
    s_ref, p_ref, m_ref, l_ref, a_ref, acc_ref, madd_ref = (next(it) for _ in range(7))
    cqb_ref = next(it) if has_bias else None
    rc = min(ROW_CHUNK, tq)

    step = pl.program_id(2)
    qi = tab_ref[step]
    kt = tab_ref[nsteps + step]

    @pl.when(tab_ref[2 * nsteps + step] == 1)
    def _init():
        for qr, qs, dk in zip(q_refs, qs_refs, dks):
            for h in range(H):
                qs[h * tq:(h + 1) * tq, :] = qr[0, :, h * dk:(h + 1) * dk].astype(BF16)
        if has_bias:
            for h in range(H):
                cqb_ref[h * tq:(h + 1) * tq, :] = jnp.broadcast_to(cq_ref[0, 0, :, h:h + 1] * LOG2E, (tq, LANES))
        m_ref[...] = jnp.full(m_ref.shape, NEG, F32)
        l_ref[...] = jnp.zeros(l_ref.shape, F32)
        acc_ref[...] = jnp.zeros(acc_ref.shape, F32)

    def build_mask():
        qpos = qi * tq + lax.broadcasted_iota(jnp.int32, (tq, tk), 0)
        kpos = kt * tk + lax.broadcasted_iota(jnp.int32, (tq, tk), 1)
        mask = kpos <= qpos
        if window:
            mask = mask & ((qpos - kpos) < window)
        if has_bm:
            bm = bm_ref[0].astype(BF16)
            nc = bm.shape[1]
            cidx = lax.broadcasted_iota(jnp.int32, (nc, tk), 0)
            kk = kt * tk + lax.broadcasted_iota(jnp.int32, (nc, tk), 1)
            expand = jnp.where(cidx == 2 * lax.shift_right_logical(kk, SEL_SHIFT), 1.0, 0.0).astype(BF16)
            mask = mask & (jnp.dot(bm, expand, preferred_element_type=F32) > 0.5)
        madd_ref[...] = jnp.where(mask, 0.0, NEG)

    def softmax_pass(masked):
        for h in range(H):
            if has_bias:
                ck_row = ck_ref[0, 0, h:h + 1, :] * LOG2E
            for r in range(tq // rc):
                pr = slice(r * rc, (r + 1) * rc)
                rows = slice(h * tq + r * rc, h * tq + (r + 1) * rc)
                t = s_ref[rows, :] * c2
                if has_bias:
                    t = (t + jnp.tile(cqb_ref[rows, :], (1, tk // LANES))) - ck_row
                if masked:
                    t = t + madd_ref[pr, :]
                s_ref[rows, :] = t
                m_prev = m_ref[rows, :]
                m_new = jnp.maximum(m_prev, jnp.max(t, axis=1, keepdims=True))
                a_ref[rows, :] = jnp.exp2(m_prev - m_new)
                m_ref[rows, :] = m_new
        for h in range(H):
            for r in range(tq // rc):
                rows = slice(h * tq + r * rc, h * tq + (r + 1) * rc)
                p = jnp.exp2(s_ref[rows, :] - jnp.tile(m_ref[rows, :], (1, tk // LANES)))
                l_ref[rows, :] = a_ref[rows, :] * l_ref[rows, :] + jnp.sum(p, axis=1, keepdims=True)
                p_ref[rows, :] = p.astype(BF16)

    def _step():
        kbs = [kr[0].astype(BF16) for kr in k_refs]
        s_full = None
        for qs, kb in zip(qs_refs, kbs):
            d = _dot_nt(qs[...], kb)
            s_full = d if s_full is None else s_full + d
        s_ref[...] = s_full
        if has_bm:
            build_mask()
            softmax_pass(True)
        else:
            visible = kt * tk + tk - 1 <= qi * tq
            if window:
                visible = visible & (qi * tq + tq - 1 - kt * tk < window)

            @pl.when(visible)
            def _():
                softmax_pass(False)

            @pl.when(jnp.logical_not(visible))
            def _():
                build_mask()
                softmax_pass(True)

        vb = kbs[0] if v_is_k0 else v_ref[0].astype(BF16)
        pv = jnp.dot(p_ref[...], vb, preferred_element_type=F32)
        acc_ref[...] = acc_ref[...] * jnp.tile(a_ref[...], (1, dv // LANES)) + pv

    _step()

    @pl.when(tab_ref[3 * nsteps + step] == 1)
    def _fin():
        for h in range(H):
            rows = slice(h * tq, (h + 1) * tq)
            ok = m_ref[rows, :] > 0.5 * NEG
            inv = jnp.where(ok, 1.0 / jnp.where(ok, l_ref[rows, :], 1.0), 0.0)
            o_ref[0, :, h * dv:(h + 1) * dv] = (acc_ref[rows, :] * jnp.tile(inv, (1, dv // LANES))).astype(o_ref.dtype)


def flash(qs, ks, v, *, H, G=1, dks, dv, scale, q_cols=None, k_cols=None, v_col=0, window=0,
          cq=None, ck=None, bm=None, tq=128, tk=512, out_dtype=F32):
    B, S = qs[0].shape[:2]
    n = len(dks)
    q_cols = q_cols or [0] * n
    k_cols = k_cols or [0] * n
    tq = min(tq, S)
    tk = min(tk, S)
    assert tk % LANES == 0 and dv % LANES == 0
    nq = S // tq
    pairs = [(i, t) for i in range(nq) for t in range(_kv_lo(i, tq, tk, window), _kv_hi(i, tq, tk) + 1)]
    nsteps = len(pairs)
    tab = np.array([[i for i, _ in pairs], [t for _, t in pairs],
                    [int(t == _kv_lo(i, tq, tk, window)) for i, t in pairs],
                    [int(t == _kv_hi(i, tq, tk)) for i, t in pairs]], np.int32).reshape(-1)

    def qblk(s, tab):
        return tab[s]

    def kblk(s, tab):
        return tab[nsteps + s]

    in_specs, args = [], []
    for a, dk, c in zip(qs, dks, q_cols):
        in_specs.append(pl.BlockSpec((1, tq, H * dk), lambda b, g, s, tab, c=c: (b, qblk(s, tab), c + g)))
        args.append(a)
    for a, dk, c in zip(ks, dks, k_cols):
        in_specs.append(pl.BlockSpec((1, tk, dk), lambda b, g, s, tab, c=c: (b, kblk(s, tab), c + g)))
        args.append(a)
    if v is not None:
        in_specs.append(pl.BlockSpec((1, tk, dv), lambda b, g, s, tab: (b, kblk(s, tab), v_col + g)))
        args.append(v)
    if cq is not None:
        in_specs.append(pl.BlockSpec((1, 1, tq, H), lambda b, g, s, tab: (b, g, qblk(s, tab), 0)))
        in_specs.append(pl.BlockSpec((1, 1, H, tk), lambda b, g, s, tab: (b, g, 0, kblk(s, tab))))
        args += [cq, ck]
    if bm is not None:
        in_specs.append(pl.BlockSpec((1, tq, bm.shape[-1]), lambda b, g, s, tab: (b, qblk(s, tab), 0)))
        args.append(bm)
    R = H * tq
    scratch = [pltpu.VMEM((R, dk), BF16) for dk in dks]
    scratch += [pltpu.VMEM((R, tk), F32), pltpu.VMEM((R, tk), BF16),
                pltpu.VMEM((R, LANES), F32), pltpu.VMEM((R, LANES), F32), pltpu.VMEM((R, LANES), F32),
                pltpu.VMEM((R, dv), F32), pltpu.VMEM((tq, tk), F32)]
    if cq is not None:
        scratch.append(pltpu.VMEM((R, LANES), F32))
    return pl.pallas_call(
        functools.partial(_flash_body, H=H, tq=tq, tk=tk, dks=tuple(dks), dv=dv, c2=scale * LOG2E,
                          window=window, has_bias=cq is not None, has_bm=bm is not None,
                          v_is_k0=v is None, nsteps=nsteps),
        grid_spec=pltpu.PrefetchScalarGridSpec(
            num_scalar_prefetch=1, grid=(B, G, nsteps), in_specs=in_specs,
            out_specs=pl.BlockSpec((1, tq, H * dv), lambda b, g, s, tab: (b, qblk(s, tab), g)),
            scratch_shapes=scratch),
        out_shape=jax.ShapeDtypeStruct((B, S, G * H * dv), out_dtype),
        compiler_params=_cparams("parallel", "parallel", "arbitrary"),
        name="flash",
    )(jnp.asarray(tab), *args)


def _cmp_topk_body(q_ref, km_ref, vm_ref, oc_ref, bm_ref, p_ref, *, H, tq, ncp, pos0, tvalid, stacked):
    qi = pl.program_id(1)
    if stacked:
        qs = q_ref[0].astype(BF16)
    else:
        qs = jnp.concatenate([q_ref[0, :, h * NSA_DH:(h + 1) * NSA_DH] for h in range(H)],
                             axis=0).astype(BF16)
    s_full = _dot_nt(qs, km_ref[0].astype(BF16))
    r = lax.broadcasted_iota(jnp.int32, (tq, ncp), 0)
    c = lax.broadcasted_iota(jnp.int32, (tq, ncp), 1)
    if tvalid is None:
        pos = pos0 + qi * tq + r
    else:
        pos = pos0 + jnp.minimum(r, tvalid - 1)
    cmask = (c + 1) * NSA_CMP - 1 <= pos
    imp = jnp.zeros((tq, ncp), F32)
    for h in range(H):
        rows = slice(h * tq, (h + 1) * tq)
        s = jnp.where(cmask, s_full[rows] * NSA_SCALE, NEG)
        m = jnp.max(s, axis=-1, keepdims=True)
        p = jnp.where(cmask, jnp.exp(s - m), 0.0)
        p = p / jnp.maximum(jnp.sum(p, axis=-1, keepdims=True), 1e-30)
        imp = imp + p
        p_ref[rows] = p.astype(BF16)
    o = jnp.dot(p_ref[...], vm_ref[0].astype(BF16), preferred_element_type=F32)
    if stacked:
        oc_ref[0] = o.astype(oc_ref.dtype)
    else:
        for h in range(H):
            oc_ref[0, :, h * NSA_DH:(h + 1) * NSA_DH] = o[h * tq:(h + 1) * tq].astype(oc_ref.dtype)
    nxt = pltpu.roll(imp, ncp - 1, axis=1)
    prv = pltpu.roll(imp, 1, axis=1)
    ps = imp + jnp.where((c & 1) == 0, nxt, prv)
    blk = lax.shift_right_logical(c, 1)
    avail = blk * NSA_SEL <= pos
    cur = lax.shift_right_logical(pos, SEL_SHIFT)
    forced = (blk == 0) | (blk == cur) | (blk == cur - 1)
    sc = jnp.where(avail, ps + jnp.where(forced, NSA_FORCE, 0.0), NEG)
    rank = jnp.zeros((tq, ncp), F32)
    for i in range(ncp // 2):
        col = sc[:, 2 * i:2 * i + 1]
        beats = (col > sc) | ((col == sc) & (blk > i))
        rank = rank + jnp.where(beats, 1.0, 0.0)
    bm_ref[0] = jnp.where(avail & (rank < NSA_TOPK), 1.0, 0.0)


def cmp_topk(q, kmean, vmean, *, tq, pos0, tvalid=None, stacked=False):
    B = q.shape[0]
    ncp = kmean.shape[1]
    H = NSA_H
    if stacked:
        nq = 1
        q_spec = pl.BlockSpec((1, H * tq, NSA_DH), lambda b, i: (b, 0, 0))
        o_spec = pl.BlockSpec((1, H * tq, NSA_DH), lambda b, i: (b, 0, 0))
        o_shape = jax.ShapeDtypeStruct((B, H * tq, NSA_DH), F32)
    else:
        nq = q.shape[1] // tq
        q_spec = pl.BlockSpec((1, tq, H * NSA_DH), lambda b, i: (b, i, 0))
        o_spec = pl.BlockSpec((1, tq, H * NSA_DH), lambda b, i: (b, i, 0))
        o_shape = jax.ShapeDtypeStruct((B, nq * tq, H * NSA_DH), F32)
    return pl.pallas_call(
        functools.partial(_cmp_topk_body, H=H, tq=tq, ncp=ncp, pos0=pos0, tvalid=tvalid, stacked=stacked),
        grid=(B, nq),
        in_specs=[q_spec,
                  pl.BlockSpec((1, ncp, NSA_DH), lambda b, i: (b, 0, 0)),
                  pl.BlockSpec((1, ncp, NSA_DH), lambda b, i: (b, 0, 0))],
        out_specs=[o_spec, pl.BlockSpec((1, tq, ncp), lambda b, i: (b, i, 0))],
        out_shape=[o_shape, jax.ShapeDtypeStruct((B, nq * tq, ncp), F32)],
        scratch_shapes=[pltpu.VMEM((H * tq, ncp), BF16)],
        compiler_params=_cparams("parallel", "arbitrary"),
        name="cmp_topk",
    )(q, kmean, vmean)


def _block_mean_body(x_ref, o_ref, *, rows):
    x = x_ref[0].astype(F32)
    o_ref[0] = jnp.sum(x.reshape(rows // NSA_CMP, NSA_CMP, x.shape[-1]), axis=1) * (1.0 / NSA_CMP)


def block_mean(x, *, tr=1024):
    B, S, d = x.shape
    tr = min(tr, S)
    assert S % tr == 0 and (tr // NSA_CMP) % SUBLANES == 0
    return pl.pallas_call(
        functools.partial(_block_mean_body, rows=tr),
        grid=(B, S // tr),
        in_specs=[pl.BlockSpec((1, tr, d), lambda b, i: (b, i, 0))],
        out_specs=pl.BlockSpec((1, tr // NSA_CMP, d), lambda b, i: (b, i, 0)),
        out_shape=jax.ShapeDtypeStruct((B, S // NSA_CMP, d), F32),
        compiler_params=_cparams("parallel", "arbitrary"),
        name="block_mean",
    )(x)


def _pool_mean_body(pt_ref, *refs, pp):
    o_ref = refs[-1]
    per = PAGE // NSA_CMP
    for i in range(pp):
        x = refs[i][...].astype(F32)
        o_ref[0, i * per:(i + 1) * per, :] = jnp.sum(x.reshape(per, NSA_CMP, x.shape[-1]), axis=1) * (1.0 / NSA_CMP)


def pool_block_mean(pool, layer, pt_flat, db, npages, *, pp=32):
    d = pool.shape[-1]
    pp = min(pp, npages)
    assert npages % pp == 0
    per = PAGE // NSA_CMP
    in_specs = [pl.BlockSpec((None, None, PAGE, d),
                             lambda b, c, pt, i=i: (layer, pt[b * npages + c * pp + i], 0, 0))
                for i in range(pp)]
    return pl.pallas_call(
        functools.partial(_pool_mean_body, pp=pp),
        grid_spec=pltpu.PrefetchScalarGridSpec(
            num_scalar_prefetch=1, grid=(db, npages // pp), in_specs=in_specs,
            out_specs=pl.BlockSpec((1, pp * per, d), lambda b, c, pt: (b, c, 0))),
        out_shape=jax.ShapeDtypeStruct((db, npages * per, d), F32),
        compiler_params=_cparams("parallel", "arbitrary"),
        name="pool_block_mean",
    )(pt_flat, *([pool] * pp))


def _cumsum_body(x_ref, c_ref, ct_ref, carry_ref, *, blk):
    x = x_ref[0].astype(F32)
    nh = x.shape[-1]

    @pl.when(pl.program_id(1) == 0)
    def _():
        carry_ref[...] = jnp.zeros(carry_ref.shape, F32)

    r = lax.broadcasted_iota(jnp.int32, (blk, blk), 0)
    cc = lax.broadcasted_iota(jnp.int32, (blk, blk), 1)
    tri = jnp.where(cc <= r, 1.0, 0.0)
    c = jnp.dot(tri, x, preferred_element_type=F32, precision=lax.Precision.HIGHEST) + carry_ref[...]
    carry_ref[...] = c[blk - 1:blk, :]
    c_ref[0] = c
    eye = jnp.where(lax.broadcasted_iota(jnp.int32, (nh, nh), 0) == lax.broadcasted_iota(jnp.int32, (nh, nh), 1),
                    1.0, 0.0)
    ct_ref[0] = lax.dot_general(eye, c, (((1,), (1,)), ((), ())), preferred_element_type=F32,
                                precision=lax.Precision.HIGHEST)


def cumsum_seq(x):
    B, S, nh = x.shape
    blk = min(PAGE, S)
    return pl.pallas_call(
        functools.partial(_cumsum_body, blk=blk),
        grid=(B, S // blk),
        in_specs=[pl.BlockSpec((1, blk, nh), lambda b, i: (b, i, 0))],
        out_specs=[pl.BlockSpec((1, blk, nh), lambda b, i: (b, i, 0)),
                   pl.BlockSpec((1, nh, blk), lambda b, i: (b, 0, i))],
        out_shape=[jax.ShapeDtypeStruct((B, S, nh), F32), jax.ShapeDtypeStruct((B, nh, S), F32)],
        scratch_shapes=[pltpu.VMEM((1, nh), F32)],
        compiler_params=_cparams("parallel", "arbitrary"),
        name="cumsum_seq",
    )(x)


def _cumsum_pages_body(pt_ref, *refs, npages):
    o_ref = refs[-1]
    nh = o_ref.shape[1]
    r = lax.broadcasted_iota(jnp.int32, (PAGE, PAGE), 0)
    cc = lax.broadcasted_iota(jnp.int32, (PAGE, PAGE), 1)
    tri = jnp.where(r <= cc, 1.0, 0.0)
    x = jnp.concatenate([refs[j][...].astype(F32) for j in range(npages)], axis=0)
    local = jnp.dot(x, tri, preferred_element_type=F32, precision=lax.Precision.HIGHEST)
    carry = jnp.zeros((nh, 1), F32)
    for j in range(npages):
        cj = local[j * nh:(j + 1) * nh, :]
        o_ref[0, :, j * PAGE:(j + 1) * PAGE] = cj + carry
        carry = carry + cj[:, PAGE - 1:PAGE]


def cumsum_pages(pool_t, layer, pt_flat, db, npages):
    nh = pool_t.shape[2]
    in_specs = [pl.BlockSpec((None, None, nh, PAGE), lambda b, pt, j=j: (layer, pt[b * npages + j], 0, 0))
                for j in range(npages)]
    return pl.pallas_call(
        functools.partial(_cumsum_pages_body, npages=npages),
        grid_spec=pltpu.PrefetchScalarGridSpec(
            num_scalar_prefetch=1, grid=(db,), in_specs=in_specs,
            out_specs=pl.BlockSpec((1, nh, npages * PAGE), lambda b, pt: (b, 0, 0))),
        out_shape=jax.ShapeDtypeStruct((db, nh, npages * PAGE), F32),
        compiler_params=_cparams("parallel"),
        name="cumsum_pages",
    )(pt_flat, *([pool_t] * npages))


def _decode_body(pt_ref, *refs, H, G, dks, ktr, pp, dv, c2, window, has_bias, has_bm, v_is_k0,
                 nchunks, kpos0, qpos0, tvalid):
    nparts = len(dks)
    it = iter(refs)
    q_refs = [next(it) for _ in range(nparts)]
    kp_refs = [[next(it) for _ in range(pp)] for _ in range(nparts)]
    vp_refs = kp_refs[0] if v_is_k0 else [next(it) for _ in range(pp)]
    kn_refs = [next(it) for _ in range(nparts)]
    vn_ref = kn_refs[0] if v_is_k0 else next(it)
    cq_ref = ck_ref = ckn_ref = bm_ref = bmn_ref = None
    if has_bias:
        cq_ref, ck_ref, ckn_ref = next(it), next(it), next(it)
    if has_bm:
        bm_ref, bmn_ref = next(it), next(it)
    o_ref = next(it)
    m_ref, l_ref, acc_ref = next(it), next(it), next(it)
    R = H * TPAD
    c = pl.program_id(1)

    @pl.when(c == 0)
    def _init():
        m_ref[...] = jnp.full(m_ref.shape, NEG, F32)
        l_ref[...] = jnp.zeros(l_ref.shape, F32)
        acc_ref[...] = jnp.zeros(acc_ref.shape, F32)

    def page_rows(ref, g):
        return ref[...] if G == 1 else ref[pl.ds(g, PAGE, stride=G), :]

    GH = G * H
    qbs = [[qr[0, g].astype(BF16) for qr in q_refs] for g in range(G)]

    def rows_cat(parts):
        return parts[0] if G == 1 else jnp.concatenate(parts, axis=0)

    def update(s, kpos, ck, key_ok, causal, pv):
        K = s.shape[-1]
        t3 = (s * c2).reshape(GH, TPAD, K)
        if has_bias:
            t3 = t3 + ((cq_ref[0] * LOG2E).reshape(GH, TPAD, 1) - (ck * LOG2E)[:, None, :])
        mask = key_ok
        if causal or window:
            tok = jnp.minimum(lax.broadcasted_iota(jnp.int32, (TPAD, K), 0), tvalid - 1)
            qpos = qpos0 + tok
            if causal:
                cm = kpos <= qpos
                mask = cm if mask is None else mask & cm
            if window:
                wm = (qpos - kpos) < window
                mask = wm if mask is None else mask & wm
        if mask is not None:
            t3 = jnp.where(mask[None], t3, NEG)
        t = t3.reshape(G * R, K)
        m_prev = m_ref[...]
        m_new = jnp.maximum(m_prev, jnp.max(t, axis=-1, keepdims=True))
        alpha = jnp.exp2(m_prev - m_new)
        p = jnp.exp2(t - m_new)
        l_ref[...] = alpha * l_ref[...] + jnp.sum(p, axis=-1, keepdims=True)
        m_ref[...] = m_new
        acc_ref[...] = acc_ref[...] * alpha + pv(p.astype(BF16))

    kb0 = [[] for _ in range(G)]
    s_groups = []
    for g in range(G):
        s_list = []
        for j in range(pp):
            s = None
            for i in range(nparts):
                if ktr[i]:
                    d = jnp.dot(qbs[g][i], kp_refs[i][j][...].astype(BF16), preferred_element_type=F32)
                else:
                    kb = page_rows(kp_refs[i][j], g).astype(BF16)
                    if i == 0:
                        kb0[g].append(kb)
                    d = _dot_nt(qbs[g][i], kb)
                s = d if s is None else s + d
            s_list.append(s)
        s_groups.append(jnp.concatenate(s_list, axis=1) if pp > 1 else s_list[0])
    K = pp * PAGE
    kpos = kpos0 + c * K + lax.broadcasted_iota(jnp.int32, (TPAD, K), 1)

    def pv_past(pb):
        outs = []
        for g in range(G):
            out = None
            for j in range(pp):
                vb = kb0[g][j] if v_is_k0 else page_rows(vp_refs[j], g).astype(BF16)
                d = jnp.dot(pb[g * R:(g + 1) * R, j * PAGE:(j + 1) * PAGE], vb, preferred_element_type=F32)
                out = d if out is None else out + d
            outs.append(out)
        return rows_cat(outs)

    sel_ok = None
    if has_bm:
        bmc = bm_ref[0]
        in_first = lax.broadcasted_iota(jnp.int32, (TPAD, PAGE), 1) < NSA_SEL
        sel_ok = jnp.concatenate(
            [jnp.where(in_first, bmc[:, LPP * j:LPP * j + 1], bmc[:, LPP * j + LPB:LPP * j + LPB + 1])
             for j in range(pp)], axis=1) > 0.5

    update(rows_cat(s_groups), kpos, ck_ref[0].reshape(GH, K) if has_bias else None, sel_ok, False, pv_past)

    @pl.when(c == nchunks - 1)
    def _new_and_fin():
        s_groups, vnbs = [], []
        for g in range(G):
            knb = [kn[0, :, g * dk:(g + 1) * dk].astype(BF16) for kn, dk in zip(kn_refs, dks)]
            s = None
            for i in range(nparts):
                d = _dot_nt(qbs[g][i], knb[i])
                s = d if s is None else s + d
            s_groups.append(s)
            vnbs.append(knb[0] if v_is_k0 else vn_ref[0, :, g * dv:(g + 1) * dv].astype(BF16))
        j = lax.broadcasted_iota(jnp.int32, (TPAD, TPAD), 1)

        def pv_new(pb):
            return rows_cat([jnp.dot(pb[g * R:(g + 1) * R], vnbs[g], preferred_element_type=F32)
                             for g in range(G)])

        new_ok = j < tvalid
        if has_bm:
            new_ok = new_ok & (bmn_ref[0] > 0.5)
        update(rows_cat(s_groups), qpos0 + j, ckn_ref[0].reshape(GH, TPAD) if has_bias else None,
               new_ok, True, pv_new)
        ok = m_ref[...] > 0.5 * NEG
        o = jnp.where(ok, acc_ref[...] / jnp.where(ok, l_ref[...], 1.0), 0.0)
        o_ref[0] = o.reshape(G, R, dv).astype(o_ref.dtype)


def decode(qs, pools, vpool, layer, pt_flat, npages, knews, vnew, *, H, G=1, dks, dv, scale,
           qpos0, kpos0=0, tvalid, ktr=None, window=0, cq=None, ck=None, cknew=None, bm=None, pp=32):
    DB = qs[0].shape[0]
    nparts = len(dks)
    ktr = ktr or (False,) * nparts
    pp = min(pp, npages)
    assert npages % pp == 0
    nchunks = npages // pp
    R = H * TPAD
    in_specs, args = [], []
    for a, dk in zip(qs, dks):
        in_specs.append(pl.BlockSpec((1, G, R, dk), lambda b, c, pt: (b, 0, 0, 0)))
        args.append(a)

    def page_specs(shape):
        return [pl.BlockSpec((None, None) + shape,
                             lambda b, c, pt, j=j: (layer, pt[b * npages + c * pp + j], 0, 0))
                for j in range(pp)]

    for a, dk, tr in zip(pools, dks, ktr):
        in_specs += page_specs((dk, PAGE) if tr else (PAGE * G, dk))
        args += [a] * pp
    if vpool is not None:
        in_specs += page_specs((PAGE * G, dv))
        args += [vpool] * pp
    for a, dk in zip(knews, dks):
        in_specs.append(pl.BlockSpec((1, TPAD, G * dk), lambda b, c, pt: (b, 0, 0)))
        args.append(a)
    if vnew is not None:
        in_specs.append(pl.BlockSpec((1, TPAD, G * dv), lambda b, c, pt: (b, 0, 0)))
        args.append(vnew)
    if cq is not None:
        in_specs.append(pl.BlockSpec((1, G, R, 1), lambda b, c, pt: (b, 0, 0, 0)))
        in_specs.append(pl.BlockSpec((1, G, H, pp * PAGE), lambda b, c, pt: (b, 0, 0, c)))
        in_specs.append(pl.BlockSpec((1, G, H, TPAD), lambda b, c, pt: (b, 0, 0, 0)))
        args += [cq, ck, cknew]
    if bm is not None:
        lpc = pp * LPP
        assert tvalid <= NSA_SEL and qpos0 % NSA_SEL == 0 and kpos0 == 0
        if nchunks == 1:
            in_specs.append(pl.BlockSpec((1, TPAD, bm.shape[-1]), lambda b, c, pt: (b, 0, 0)))
        else:
            assert lpc % LANES == 0
            in_specs.append(pl.BlockSpec((1, TPAD, lpc), lambda b, c, pt: (b, 0, c)))
        new_lane = LPB * (qpos0 // NSA_SEL)
        in_specs.append(pl.BlockSpec((1, TPAD, 1), lambda b, c, pt: (b, 0, 0)))
        args += [bm, bm[:, :, new_lane:new_lane + 1]]
    return pl.pallas_call(
        functools.partial(_decode_body, H=H, G=G, dks=tuple(dks), ktr=tuple(ktr), pp=pp, dv=dv,
                          c2=scale * LOG2E, window=window, has_bias=cq is not None, has_bm=bm is not None,
                          v_is_k0=vpool is None, nchunks=nchunks, kpos0=kpos0, qpos0=qpos0, tvalid=tvalid),
        grid_spec=pltpu.PrefetchScalarGridSpec(
            num_scalar_prefetch=1, grid=(DB, nchunks), in_specs=in_specs,
            out_specs=pl.BlockSpec((1, G, R, dv), lambda b, c, pt: (b, 0, 0, 0)),
            scratch_shapes=[pltpu.VMEM((G * R, 1), F32), pltpu.VMEM((G * R, 1), F32),
                            pltpu.VMEM((G * R, dv), F32)]),
        out_shape=jax.ShapeDtypeStruct((DB, G, R, dv), F32),
        compiler_params=_cparams("parallel", "arbitrary"),
        name="decode",
    )(pt_flat, *args)


def _nsa_mix_body(oc_ref, os_ref, ow_ref, gt_ref, g_ref, o_ref):
    gt = gt_ref[...]
    for h in range(NSA_H):
        cols = slice(h * NSA_DH, (h + 1) * NSA_DH)
        o = (gt[:, h:h + 1] * oc_ref[:, cols] + gt[:, NSA_H + h:NSA_H + h + 1] * os_ref[:, cols]
             + gt[:, 2 * NSA_H + h:2 * NSA_H + h + 1] * ow_ref[:, cols])
        g = g_ref[:, cols]
        o_ref[:, cols] = (o * (g * jax.nn.sigmoid(g))).astype(o_ref.dtype)


def nsa_mix(oc, osel, ow, gates, proj, g_col, *, tm=256):
    M, W = oc.shape
    tm = min(tm, M)
    assert M % tm == 0
    spec = pl.BlockSpec((tm, W), lambda i: (i, 0))
    return pl.pallas_call(
        _nsa_mix_body,
        grid=(M // tm,),
        in_specs=[spec, spec, spec, pl.BlockSpec((tm, gates.shape[1]), lambda i: (i, 0)),
                  pl.BlockSpec((tm, W), lambda i: (i, g_col))],
        out_specs=spec,
        out_shape=jax.ShapeDtypeStruct((M, W), BF16),
        compiler_params=_cparams("parallel"),
        name="nsa_mix",
    )(oc, osel, ow, gates, proj)


def _shift_append_body(buf_ref, new_ref, o_ref, *, wb, t):
    o_ref[:, 0:wb - t, :] = buf_ref[:, t:wb, :]
    o_ref[:, wb - t:wb, :] = new_ref[:, 0:t, :]


def shift_append(state, layer, new, t, *, tb=8):
    _, DB, wb, d = state.shape
    tb = min(tb, DB)
    assert DB % tb == 0
    return pl.pallas_call(
        functools.partial(_shift_append_body, wb=wb, t=t),
        grid=(DB // tb,),
        in_specs=[pl.BlockSpec((None, tb, wb, d), lambda i: (layer, i, 0, 0)),
                  pl.BlockSpec((tb, TPAD, d), lambda i: (i, 0, 0))],
        out_specs=pl.BlockSpec((tb, wb, d), lambda i: (i, 0, 0)),
        out_shape=jax.ShapeDtypeStruct((DB, wb, d), state.dtype),
        compiler_params=_cparams("parallel"),
        name="shift_append",
    )(state, new)


def _rope(x, pos):
    half = x.shape[-1] // 2
    inv = ROPE_THETA ** (-jnp.arange(half, dtype=F32) / half)
    ang = pos.astype(F32)[:, None] * inv[None, :]
    cos = jnp.cos(ang)[:, None, :]
    sin = jnp.sin(ang)[:, None, :]
    x1, x2 = x[..., :half], x[..., half:]
    return jnp.concatenate([x1 * cos - x2 * sin, x1 * sin + x2 * cos], axis=-1)


def _pad_rows(x, n, axis):
    pad = [(0, 0)] * x.ndim
    pad[axis] = (0, n - x.shape[axis])
    return jnp.pad(x, pad)


def _stack_rows(x, H, d):
    DB, T, _ = x.shape
    x = x.reshape(DB, T, H, d).transpose(0, 2, 1, 3)
    return _pad_rows(x, TPAD, 2).reshape(DB, 1, H * TPAD, d)


def _unstack_rows(o, H, T):
    DB, _, d = o.shape
    return o.reshape(DB, H, TPAD, d)[:, :, :T].transpose(0, 2, 1, 3).reshape(DB, T, H * d)


def _even_weights(w_in, w_uq, w_uk, w_uv, w_out):
    D = w_in.shape[0]
    o = np.cumsum((0, MLA_QL, MLA_KVL, MLA_ROPE, MLA_H * MLA_V, FOX_H * FOX_DH, FOX_KVH * FOX_DH,
                   FOX_KVH * FOX_DH, FOX_H, FOX_H * FOX_DH))
    seg = lambda i: w_in[:, o[i]:o[i + 1]]
    order = [seg(4), seg(3), seg(8), seg(0), seg(1), seg(5), seg(6), seg(2), seg(7)]
    n = sum(s.shape[1] for s in order)
    w = jnp.concatenate(order + [jnp.zeros((D, _round_up(n, 512) - n), w_in.dtype)], axis=1).astype(BF16)
    uq = jnp.concatenate([w_uq[:, :, :MLA_NOPE].reshape(MLA_QL, -1), w_uq[:, :, MLA_NOPE:].reshape(MLA_QL, -1)],
                         axis=1).astype(BF16)
    uk = w_uk.transpose(1, 2, 0).astype(BF16)
    uv = w_uv.transpose(1, 0, 2).astype(BF16)
    return w, uq, uk, uv, w_out.astype(BF16)


EV_COL = dict(fq=0, gm=1024, gf=2048, cq=3072, ckv=3584, fk=3840, fv=4096, kr=4352, fl=4416)


def _odd_weights(w_in, gate_bias, w_out):
    D = w_in.shape[0]
    W = NSA_H * NSA_DH
    q, kv, gl, g = w_in[:, :W], w_in[:, W:W + 6 * NSA_DH], w_in[:, W + 6 * NSA_DH:W + 6 * NSA_DH + 3 * NSA_H], \
        w_in[:, W + 6 * NSA_DH + 3 * NSA_H:]
    gl = gl.reshape(D, NSA_H, 3).transpose(0, 2, 1).reshape(D, 3 * NSA_H)
    n = 2 * W + 6 * NSA_DH + 3 * NSA_H
    w = jnp.concatenate([q, g, kv, gl, jnp.zeros((D, _round_up(n, 512) - n), w_in.dtype)], axis=1).astype(BF16)
    gb = gate_bias.reshape(NSA_H, 3).T.reshape(3 * NSA_H)
    return w, gb, w_out.astype(BF16)


def kernel(x_prompt, x_sample, cache_mla_ckv, cache_mla_krope, cache_fox_k, cache_fox_v, cache_fox_logf, cache_nsa_cmp_k, cache_nsa_cmp_v, cache_nsa_sel_k, cache_nsa_sel_v, state_nsa_win_k, state_nsa_win_v, page_table, c_prompt, c_sample, ada_w, ada_b, norm_g, final_g, ev_w_in, mla_q_norm, mla_kv_norm, mla_w_uq, mla_w_uk, mla_w_uv, fox_f_bias, ev_w_out, od_w_in, nsa_gate_bias, od_w_out):
    B, S, D = x_prompt.shape
    DB, T, _ = x_sample.shape
    depth = ada_w.shape[0]
    npages = page_table.shape[1]
    P = npages * PAGE
    MS = DB * T
    pos_p = jnp.arange(S)
    pos_s = P + jnp.arange(T)
    pt_flat = page_table.reshape(-1).astype(jnp.int32)

    c_all = jnp.concatenate([c_prompt, c_sample], axis=0)
    mc = _round_up(B + DB, SUBLANES)
    c_all = _pad_rows(c_all, mc, 0)[None]
    fox_pool_k = cache_fox_k.reshape(cache_fox_k.shape[:2] + (PAGE * FOX_KVH, FOX_DH))
    fox_pool_v = cache_fox_v.reshape(cache_fox_v.shape[:2] + (PAGE * FOX_KVH, FOX_DH))
    krope_pool_t = jnp.swapaxes(cache_mla_krope, 2, 3)
    logf_pool_t = jnp.swapaxes(cache_fox_logf, 2, 3)

    xp = x_prompt
    xs = x_sample.reshape(1, MS, D)
    outs = {k: [] for k in ('ckv_p', 'ckv_s', 'kr_p', 'kr_s', 'fk_p', 'fk_s', 'fv_p', 'fv_s', 'lf_p', 'lf_s',
                            'kc_p', 'kc_s', 'vc_p', 'vc_s', 'ks_p', 'ks_s', 'vs_p', 'vs_s',
                            'wk_p', 'wk_s', 'wv_p', 'wv_s')}

    for l in range(depth):
        mod = mm(c_all, ada_w, w_idx=l, bias=ada_b[l][None])[0]
        shift, scale, gate = mod[:, :D], mod[:, D:2 * D], mod[:, 2 * D:]
        rep = lambda a: jnp.repeat(a[B:B + DB], T, axis=0)[None]
        hp = norm(xp, norm_g[l], scale[:B, None], shift[:B, None], out_dtype=BF16)
        hs = norm(xs, norm_g[l], rep(scale), rep(shift), out_dtype=BF16)
        gate_p, gate_s = gate[:B, None], rep(gate)

        if l % 2 == 0:
            e = l // 2
            w_in, w_uq, w_uk, w_uv, w_out = _even_weights(ev_w_in[e], mla_w_uq[e], mla_w_uk[e], mla_w_uv[e],
                                                          ev_w_out[e])
            C = EV_COL

            def project(h, pos):
                proj = mm(h, w_in)
                g_, r_, _ = proj.shape
                cqn = norm(proj, mla_q_norm[e], col=C['cq'] // MLA_QL, width=MLA_QL, out_dtype=BF16)
                q = mm(cqn, w_uq)
                q_lat = mm_heads(q.reshape(g_ * r_, -1), w_uk).reshape(g_, r_, -1)
                q_rope = _rope(q[..., MLA_H * MLA_NOPE:].reshape(g_, r_, MLA_H, MLA_ROPE), pos).reshape(g_, r_, -1)
                ckv = norm(proj, mla_kv_norm[e], col=C['ckv'] // MLA_KVL, width=MLA_KVL)
                kr = _rope(proj[..., None, C['kr']:C['kr'] + MLA_ROPE], pos)[..., 0, :]
                logf = jax.nn.log_sigmoid(proj[..., C['fl']:C['fl'] + FOX_H] + fox_f_bias[e])
                return proj, q_lat, q_rope, ckv, kr, logf

            def output(proj, o_lat, o_fox, x, gate_):
                g_, r_, _ = proj.shape
                o_mla = mm_heads(o_lat.reshape(g_ * r_, -1), w_uv).reshape(g_, r_, -1)
                gm = proj[..., C['gm']:C['gm'] + MLA_H * MLA_V]
                gf = proj[..., C['gf']:C['gf'] + FOX_H * FOX_DH]
                mix = jnp.concatenate([o_mla * jax.nn.silu(gm), o_fox * jax.nn.silu(gf)], axis=-1).astype(BF16)
                return mm(mix, w_out, res=x, gate=gate_)

            proj, q_lat, q_rope, ckv, kr, logf = project(hp, pos_p)
            o_lat = flash([q_lat, q_rope], [ckv, kr], None, H=MLA_H, dks=(MLA_KVL, MLA_ROPE), dv=MLA_KVL,
                          scale=MLA_SCALE, tq=256)
            csum, csum_t = cumsum_seq(logf)
            cq = csum.reshape(B, S, FOX_KVH, FOX_G).transpose(0, 2, 1, 3)
            ck = csum_t.reshape(B, FOX_KVH, FOX_G, S)
            kb = C['fk'] // FOX_DH
            vb = C['fv'] // FOX_DH
            o_fox = flash([proj], [proj], proj, H=FOX_G, G=FOX_KVH, dks=(FOX_DH,), dv=FOX_DH, scale=FOX_SCALE,
                          q_cols=[0], k_cols=[kb], v_col=vb, cq=cq, ck=ck, tq=256)
            fk = proj[..., C['fk']:C['fk'] + FOX_KVH * FOX_DH]
            fv = proj[..., C['fv']:C['fv'] + FOX_KVH * FOX_DH]
            outs['ckv_p'].append(ckv)
            outs['kr_p'].append(kr)
            outs['fk_p'].append(fk.reshape(B, S, FOX_KVH, FOX_DH))
            outs['fv_p'].append(fv.reshape(B, S, FOX_KVH, FOX_DH))
            outs['lf_p'].append(logf)
            xp = output(proj, o_lat, o_fox, xp, gate_p)

            proj, q_lat, q_rope, ckv, kr, logf = project(hs, jnp.tile(pos_s, DB))
            ckv3, kr3 = ckv.reshape(DB, T, -1), kr.reshape(DB, T, -1)
            o_lat = decode([_stack_rows(q_lat.reshape(DB, T, -1), MLA_H, MLA_KVL),
                            _stack_rows(q_rope.reshape(DB, T, -1), MLA_H, MLA_ROPE)],
                           [cache_mla_ckv, krope_pool_t], None, e, pt_flat, npages,
                           [_pad_rows(ckv3, TPAD, 1), _pad_rows(kr3, TPAD, 1)], None,
                           H=MLA_H, dks=(MLA_KVL, MLA_ROPE), ktr=(False, True), dv=MLA_KVL, scale=MLA_SCALE,
                           qpos0=P, tvalid=T, pp=64)
            o_lat = _unstack_rows(o_lat[:, 0], MLA_H, T).reshape(1, MS, -1)
            cpast_t = cumsum_pages(logf_pool_t, e, pt_flat, DB, npages)
            lf3 = logf.reshape(DB, T, FOX_H)
            c_new = cpast_t[:, :, -1][:, None, :] + jnp.cumsum(lf3, axis=1)
            fq = proj[0, :, :FOX_H * FOX_DH].reshape(DB, T, FOX_KVH, FOX_G, FOX_DH).transpose(0, 2, 3, 1, 4)
            fq = _pad_rows(fq, TPAD, 3).reshape(DB, FOX_KVH, FOX_G * TPAD, FOX_DH)
            cqs = _pad_rows(c_new.reshape(DB, T, FOX_KVH, FOX_G).transpose(0, 2, 3, 1), TPAD, 3)
            fk = proj[0, :, C['fk']:C['fk'] + FOX_KVH * FOX_DH].reshape(DB, T, -1)
            fv = proj[0, :, C['fv']:C['fv'] + FOX_KVH * FOX_DH].reshape(DB, T, -1)
            o_fox = decode([fq], [fox_pool_k], fox_pool_v, e, pt_flat, npages,
                           [_pad_rows(fk, TPAD, 1)], _pad_rows(fv, TPAD, 1),
                           H=FOX_G, G=FOX_KVH, dks=(FOX_DH,), dv=FOX_DH, scale=FOX_SCALE, qpos0=P, tvalid=T,
                           cq=cqs.reshape(DB, FOX_KVH, FOX_G * TPAD, 1),
                           ck=cpast_t.reshape(DB, FOX_KVH, FOX_G, P), cknew=cqs)
            o_fox = o_fox.reshape(DB, FOX_KVH, FOX_G, TPAD, FOX_DH)[:, :, :, :T].transpose(0, 3, 1, 2, 4)
            o_fox = o_fox.reshape(1, MS, FOX_H * FOX_DH)
            outs['ckv_s'].append(ckv3)
            outs['kr_s'].append(kr3)
            outs['fk_s'].append(fk.reshape(DB, T, FOX_KVH, FOX_DH))
            outs['fv_s'].append(fv.reshape(DB, T, FOX_KVH, FOX_DH))
            outs['lf_s'].append(lf3)
            xs = output(proj, o_lat, o_fox, xs, gate_s)
        else:
            od = l // 2
            w_in, gbias, w_out = _odd_weights(od_w_in[od], nsa_gate_bias[od], od_w_out[od])
            W = NSA_H * NSA_DH
            KV0 = 2 * W

            def project(h, pos):
                proj = mm(h, w_in)
                g_, r_, _ = proj.shape
                q = _rope(proj[..., :W].reshape(g_, r_, NSA_H, NSA_DH), pos).reshape(g_, r_, W)
                kv = [proj[..., KV0 + i * NSA_DH:KV0 + (i + 1) * NSA_DH] for i in range(6)]
                kc, vc, ks, vs, kw, vw = kv
                kc, ks, kw = (_rope(k[..., None, :], pos)[..., 0, :] for k in (kc, ks, kw))
                gates = jax.nn.sigmoid(proj[..., KV0 + 6 * NSA_DH:KV0 + 6 * NSA_DH + 3 * NSA_H] + gbias)
                return proj, q, kc, vc, ks, vs, kw, vw, gates

            def output(proj, o_c, o_s, o_w, gates, x, gate_):
                g_, r_, _ = proj.shape
                mix = nsa_mix(o_c.reshape(g_ * r_, W), o_s.reshape(g_ * r_, W), o_w.reshape(g_ * r_, W),
                              gates.reshape(g_ * r_, -1), proj.reshape(g_ * r_, -1), 1)
                return mm(mix.reshape(g_, r_, W), w_out, res=x, gate=gate_)

            proj, q, kc, vc, ks, vs, kw, vw, gates = project(hp, pos_p)
            ncp = _round_up(S // NSA_CMP, LANES)
            kmean = _pad_rows(block_mean(kc, tr=min(1024, S)), ncp, 1)
            vmean = _pad_rows(block_mean(vc, tr=min(1024, S)), ncp, 1)
            tq = min(128, S)
            o_c, bm = cmp_topk(q, kmean, vmean, tq=tq, pos0=0)
            o_s = flash([q], [ks], vs, H=NSA_H, dks=(NSA_DH,), dv=NSA_DH, scale=NSA_SCALE, bm=bm, tq=256)
            o_w = flash([q], [kw], vw, H=NSA_H, dks=(NSA_DH,), dv=NSA_DH, scale=NSA_SCALE, window=NSA_WIN,
                        tq=tq, tk=tq)
            wbp = min(NSA_WIN, S)
            for n_, v_ in zip(('kc_p', 'vc_p', 'ks_p', 'vs_p', 'wk_p', 'wv_p'),
                              (kc, vc, ks, vs, kw[:, S - wbp:], vw[:, S - wbp:])):
                outs[n_].append(v_)
            xp = output(proj, o_c, o_s, o_w, gates, xp, gate_p)

            proj, q, kc, vc, ks, vs, kw, vw, gates = project(hs, jnp.tile(pos_s, DB))
            r3 = lambda a: a.reshape(DB, T, -1)
            kc, vc, ks, vs, kw, vw = map(r3, (kc, vc, ks, vs, kw, vw))
            qst = _stack_rows(r3(q), NSA_H, NSA_DH)
            ncp = _round_up((P + T + NSA_SEL - 1) // NSA_SEL * 2, LANES)
            kmean = _pad_rows(pool_block_mean(cache_nsa_cmp_k, od, pt_flat, DB, npages), ncp, 1)
            vmean = _pad_rows(pool_block_mean(cache_nsa_cmp_v, od, pt_flat, DB, npages), ncp, 1)
            o_c, bm = cmp_topk(qst[:, 0], kmean, vmean, tq=TPAD, pos0=P, tvalid=T, stacked=True)
            pad8 = lambda a: _pad_rows(a, TPAD, 1)
            o_s = decode([qst], [cache_nsa_sel_k], cache_nsa_sel_v, od, pt_flat, npages, [pad8(ks)], pad8(vs),
                         H=NSA_H, dks=(NSA_DH,), dv=NSA_DH, scale=NSA_SCALE, qpos0=P, tvalid=T, bm=bm, pp=64)
            wb = state_nsa_win_k.shape[2]
            wpages = wb // PAGE
            wtab = jnp.arange(DB * wpages, dtype=jnp.int32)
            wshape = (state_nsa_win_k.shape[0], DB * wpages, PAGE, NSA_DH)
            o_w = decode([qst], [state_nsa_win_k.reshape(wshape)], state_nsa_win_v.reshape(wshape), od, wtab,
                         wpages, [pad8(kw)], pad8(vw), H=NSA_H, dks=(NSA_DH,), dv=NSA_DH, scale=NSA_SCALE,
                         qpos0=P, kpos0=P - wb, tvalid=T, window=NSA_WIN, pp=wpages)
            un = lambda o: _unstack_rows(o, NSA_H, T).reshape(1, MS, W)
            bk = shift_append(state_nsa_win_k, od, pad8(kw), T)
            bv = shift_append(state_nsa_win_v, od, pad8(vw), T)
            for n_, v_ in zip(('kc_s', 'vc_s', 'ks_s', 'vs_s', 'wk_s', 'wv_s'), (kc, vc, ks, vs, bk, bv)):
                outs[n_].append(v_)
            xs = output(proj, un(o_c), un(o_s[:, 0]), un(o_w[:, 0]), gates, xs, gate_s)

    y_prompt = norm(xp, final_g)
    y_sample = norm(xs, final_g).reshape(DB, T, D)
    st = {k: jnp.stack(v) for k, v in outs.items()}
    return (y_prompt, y_sample, st['ckv_p'], st['ckv_s'], st['kr_p'], st['kr_s'],
            st['fk_p'], st['fk_s'], st['fv_p'], st['fv_s'], st['lf_p'], st['lf_s'],
            st['kc_p'], st['kc_s'], st['vc_p'], st['vc_s'], st['ks_p'], st['ks_s'],
            st['vs_p'], st['vs_s'], st['wk_p'], st['wk_s'], st['wv_p'], st['wv_s'])
```
